```python
import math
import jax, jax.numpy as jnp
from jax import lax
import numpy as np

D_MODEL = 2048
BATCH = 4
SEQ = 4096
DEPTH = 4

N_MIXERS = 3
D_FF = 4 * D_MODEL
D_PLE = 256
NORM_EPS = 1e-6
LN_EPS = 1e-5
NEG = -1e30

NSA_HEADS = 16
NSA_KV_HEADS = 4
NSA_HPG = NSA_HEADS // NSA_KV_HEADS
NSA_HEAD_DIM = D_MODEL // NSA_HEADS
NSA_CMP_LEN = 32
NSA_CMP_STRIDE = 16
NSA_CMP_HIDDEN = NSA_HEAD_DIM
NSA_SEL_LEN = 64
NSA_SEL_TOPK = 16
NSA_SEL_LOCAL = 2
NSA_WINDOW = 512
NSA_Q_BLOCK = 32
NSA_FORCE = 1e9
NSA_IN_WIDTH = NSA_HEADS * NSA_HEAD_DIM + 6 * NSA_KV_HEADS * NSA_HEAD_DIM + 3 * NSA_HEADS

CONV_WIDTH = 31
CONV_INNER = D_MODEL

GLA_HEADS = 4
GLA_DK = D_MODEL // 2 // GLA_HEADS
GLA_DV = D_MODEL // GLA_HEADS
GLA_GATE_RANK = 16
GLA_GATE_NORM = 16.0
GLA_CHUNK = 64
GLA_IN_WIDTH = 2 * GLA_HEADS * GLA_DK + 2 * GLA_HEADS * GLA_DV + GLA_GATE_RANK

kernel_name = "hybrid_nsa_conformer_gla_trunk"


def rms_norm(x, gain):
    xf = x.astype(jnp.float32)
    y = xf * lax.rsqrt(jnp.mean(xf * xf, axis=-1, keepdims=True) + NORM_EPS)
    return (y * gain.astype(jnp.float32)).astype(x.dtype)


def layer_norm(x, gain, bias):
    xf = x.astype(jnp.float32)
    mu = jnp.mean(xf, axis=-1, keepdims=True)
    xc = xf - mu
    y = xc * lax.rsqrt(jnp.mean(xc * xc, axis=-1, keepdims=True) + LN_EPS)
    return (y * gain.astype(jnp.float32) + bias.astype(jnp.float32)).astype(x.dtype)


def masked_softmax(s, valid):
    s = jnp.where(valid, s.astype(jnp.float32), NEG)
    return jax.nn.softmax(s, axis=-1) * valid


def nsa_mixer(h, w_in, w_out, pos_k, w1_k, w2_k, pos_v, w1_v, w2_v):
    B, S, _ = h.shape
    G, HPG, Dh = NSA_KV_HEADS, NSA_HPG, NSA_HEAD_DIM
    kvw = G * Dh
    proj = h @ w_in
    cuts = [NSA_HEADS * Dh + i * kvw for i in range(7)]
    q, kc, vc, ks, vs, kw, vw, gates = jnp.split(proj, cuts, axis=-1)
    q = q.reshape(B, S, G, HPG, Dh) * (Dh ** -0.5)
    kc, vc, ks, vs, kw, vw = (a.reshape(B, S, G, Dh) for a in (kc, vc, ks, vs, kw, vw))
    gates = jax.nn.sigmoid(gates.reshape(B, S, G, HPG, 3))

    n_cmp = (S - NSA_CMP_LEN) // NSA_CMP_STRIDE + 1
    cmp_idx = np.arange(n_cmp)[:, None] * NSA_CMP_STRIDE + np.arange(NSA_CMP_LEN)[None, :]
    cmp_end = cmp_idx[:, -1]

    def compress(kv, pos, w1, w2):
        blk = kv[:, cmp_idx] + pos[None, None, :, None, :]
        blk = blk.transpose(0, 1, 3, 2, 4).reshape(B, n_cmp, G, NSA_CMP_LEN * Dh)
        return jax.nn.silu(blk @ w1) @ w2

    k_cmp = compress(kc, pos_k, w1_k, w2_k)
    v_cmp = compress(vc, pos_v, w1_v, w2_v)

    n_sel = S // NSA_SEL_LEN
    n_top = min(NSA_SEL_TOPK, n_sel)
    cmp_start = np.arange(n_cmp) * NSA_CMP_STRIDE
    sel_start = np.arange(n_sel) * NSA_SEL_LEN
    overlap = jnp.asarray(((cmp_start[:, None] <= sel_start[None, :] + NSA_SEL_LEN - 1)
                           & (cmp_start[:, None] + NSA_CMP_LEN - 1 >= sel_start[None, :])).astype(np.float32))
    ks_blk = ks.reshape(B, n_sel, NSA_SEL_LEN, G, Dh).transpose(0, 3, 1, 2, 4)
    vs_blk = vs.reshape(B, n_sel, NSA_SEL_LEN, G, Dh).transpose(0, 3, 1, 2, 4)
    b_ix = jnp.arange(B)[:, None, None, None]
    g_ix = jnp.arange(G)[None, :, None, None]
    blk_ids = jnp.arange(n_sel)
    tok_off = jnp.arange(NSA_SEL_LEN)

    pad = ((0, 0), (NSA_WINDOW, 0), (0, 0), (0, 0))
    kw_pad = jnp.pad(kw, pad)
    vw_pad = jnp.pad(vw, pad)
    QB = NSA_Q_BLOCK

    def block(start):
        t = start + jnp.arange(QB)
        qb = lax.dynamic_slice_in_dim(q, start, QB, axis=1)
        gb = lax.dynamic_slice_in_dim(gates, start, QB, axis=1)

        valid_c = cmp_end[None, :] <= t[:, None]
        p_c = masked_softmax(jnp.einsum('bqghd,bngd->bghqn', qb, k_cmp), valid_c)
        o_c = jnp.einsum('bghqn,bngd->bqghd', p_c.astype(v_cmp.dtype), v_cmp)

        imp = jnp.einsum('bghqn,nm->bgqm', p_c, overlap)
        diff = (t // NSA_SEL_LEN)[:, None] - blk_ids[None, :]
        forced = (blk_ids[None, :] == 0) | ((diff >= 0) & (diff < NSA_SEL_LOCAL))
        imp = jnp.where(forced, NSA_FORCE, imp)
        imp = jnp.where(diff >= 0, imp, -NSA_FORCE)
        _, sel = lax.top_k(imp, n_top)
        k_sel = ks_blk[b_ix, g_ix, sel]
        v_sel = vs_blk[b_ix, g_ix, sel]
        tok = sel[..., None] * NSA_SEL_LEN + tok_off
        valid_s = (tok <= t[None, None, :, None, None]).reshape(B, G, 1, QB, n_top * NSA_SEL_LEN)
        s_s = jnp.einsum('bqghd,bgqnld->bghqnl', qb, k_sel).reshape(B, G, HPG, QB, n_top * NSA_SEL_LEN)
        p_s = masked_softmax(s_s, valid_s)
        o_s = jnp.einsum('bghqk,bgqkd->bqghd', p_s.astype(v_sel.dtype),
                         v_sel.reshape(B, G, QB, n_top * NSA_SEL_LEN, Dh))

        kwb = lax.dynamic_slice_in_dim(kw_pad, start, QB + NSA_WINDOW, axis=1)
        vwb = lax.dynamic_slice_in_dim(vw_pad, start, QB + NSA_WINDOW, axis=1)
        kpos = start - NSA_WINDOW + jnp.arange(QB + NSA_WINDOW)
        valid_w = ((kpos[None, :] <= t[:, None]) & (kpos[None, :] > t[:, None] - NSA_WINDOW)
                   & (kpos[None, :] >= 0))
        p_w = masked_softmax(jnp.einsum('bqghd,bkgd->bghqk', qb, kwb), valid_w)
        o_w = jnp.einsum('bghqk,bkgd->bqghd', p_w.astype(vwb.dtype), vwb)

        return gb[..., 0, None] * o_c + gb[..., 1, None] * o_s + gb[..., 2, None] * o_w

    out = lax.map(block, jnp.arange(S // QB) * QB)
    out = out.transpose(1, 0, 2, 3, 4, 5).reshape(B, S, NSA_HEADS * Dh)
    return out @ w_out


def conv_mixer(h, w_in, dw, db, ln_g, ln_b, w_out):
    a, g = jnp.split(h @ w_in, 2, axis=-1)
    u = a * jax.nn.sigmoid(g)
    y = lax.conv_general_dilated(
        u, dw[:, None, :], window_strides=(1,), padding=[(CONV_WIDTH - 1, 0)],
        dimension_numbers=('NWC', 'WIO', 'NWC'), feature_group_count=CONV_INNER) + db
    y = jax.nn.silu(layer_norm(y, ln_g, ln_b))
    return y @ w_out


def gla_mixer(h, w_in, w_gate_up, b_gate, norm_g, w_out):
    B, S, _ = h.shape
    H, dk, dv, C = GLA_HEADS, GLA_DK, GLA_DV, GLA_CHUNK
    nC = S // C
    cuts = [H * dk, 2 * H * dk, 2 * H * dk + H * dv, 2 * H * dk + 2 * H * dv]
    q, k, v, r, gz = jnp.split(h @ w_in, cuts, axis=-1)
    glog = jax.nn.log_sigmoid((gz @ w_gate_up + b_gate).astype(jnp.float32)) / GLA_GATE_NORM
    f32 = jnp.float32
    q = q.astype(f32).reshape(B, nC, C, H, dk) * (dk ** -0.5)
    k = k.astype(f32).reshape(B, nC, C, H, dk)
    v = v.astype(f32).reshape(B, nC, C, H, dv)
    bcum = jnp.cumsum(glog.reshape(B, nC, C, H, dk), axis=2)
    blast = bcum[:, :, -1]
    q_dec = q * jnp.exp(bcum)
    k_intra = k * jnp.exp(-bcum)
    k_state = k * jnp.exp(blast[:, :, None] - bcum)
    causal = jnp.asarray(np.tril(np.ones((C, C), dtype=bool)))
    A = jnp.where(causal, jnp.einsum('bncha,bnsha->bnhcs', q_dec, k_intra), 0.0)
    o_intra = jnp.einsum('bnhcs,bnshv->bnchv', A, v)

    def step(state, xs):
        qd, kst, vv, dl = xs
        o = jnp.einsum('bcha,bhav->bchv', qd, state)
        state = state * jnp.exp(dl)[..., None] + jnp.einsum('bcha,bchv->bhav', kst, vv)
        return state, o

    xs = tuple(jnp.moveaxis(a, 1, 0) for a in (q_dec, k_state, v, blast))
    _, o_inter = lax.scan(step, jnp.zeros((B, H, dk, dv), f32), xs)
    o = (o_intra + jnp.moveaxis(o_inter, 0, 1)).reshape(B, S, H, dv)
    o = o * lax.rsqrt(jnp.mean(o * o, axis=-1, keepdims=True) + NORM_EPS) * norm_g.astype(f32)
    o = o.reshape(B, S, H * dv).astype(h.dtype) * jax.nn.silu(r)
    return o @ w_out


def setup_inputs(seed: int = 0) -> dict:
    key = jax.random.key(seed)
    keys = iter(jax.random.split(key, 64))

    def nrm(shape, scale):
        return jax.random.normal(next(keys), shape, jnp.float32) * scale

    def gain(shape):
        return 1.0 + nrm(shape, 0.02)

    n_nsa = len(range(0, DEPTH, N_MIXERS))
    n_conv = len(range(1, DEPTH, N_MIXERS))
    n_gla = len(range(2, DEPTH, N_MIXERS))
    D, Dh, L = D_MODEL, NSA_HEAD_DIM, NSA_CMP_LEN
    return {
        "x": nrm((BATCH, SEQ, D), 1.0),
        "p": nrm((DEPTH, BATCH, SEQ, D_PLE), 1.0),
        "norm_mix": gain((DEPTH, D)),
        "norm_ffn": gain((DEPTH, D)),
        "norm_ple": gain((DEPTH, D)),
        "norm_final": gain((D,)),
        "ffn_w1": nrm((DEPTH, D, D_FF), D ** -0.5),
        "ffn_w2": nrm((DEPTH, D_FF, D), D_FF ** -0.5),
        "ple_w_proj": nrm((DEPTH, D_PLE, D), D_PLE ** -0.5),
        "ple_w_gate": nrm((DEPTH, D, D), D ** -0.5),
        "nsa_w_in": nrm((n_nsa, D, NSA_IN_WIDTH), D ** -0.5),
        "nsa_w_out": nrm((n_nsa, NSA_HEADS * Dh, D), (NSA_HEADS * Dh) ** -0.5),
        "nsa_cmp_pos_k": nrm((n_nsa, L, Dh), 0.02),
        "nsa_cmp_w1_k": nrm((n_nsa, L * Dh, NSA_CMP_HIDDEN), (L * Dh) ** -0.5),
        "nsa_cmp_w2_k": nrm((n_nsa, NSA_CMP_HIDDEN, Dh), NSA_CMP_HIDDEN ** -0.5),
        "nsa_cmp_pos_v": nrm((n_nsa, L, Dh), 0.02),
        "nsa_cmp_w1_v": nrm((n_nsa, L * Dh, NSA_CMP_HIDDEN), (L * Dh) ** -0.5),
        "nsa_cmp_w2_v": nrm((n_nsa, NSA_CMP_HIDDEN, Dh), NSA_CMP_HIDDEN ** -0.5),
        "conv_w_in": nrm((n_conv, D, 2 * CONV_INNER), D ** -0.5),
        "conv_dw": nrm((n_conv, CONV_WIDTH, CONV_INNER), CONV_WIDTH ** -0.5),
        "conv_db": nrm((n_conv, CONV_INNER), 0.01),
        "conv_ln_g": gain((n_conv, CONV_INNER)),
        "conv_ln_b": nrm((n_conv, CONV_INNER), 0.01),
        "conv_w_out": nrm((n_conv, CONV_INNER, D), CONV_INNER ** -0.5),
        "gla_w_in": nrm((n_gla, D, GLA_IN_WIDTH), D ** -0.5),
        "gla_w_gate_up": nrm((n_gla, GLA_GATE_RANK, GLA_HEADS * GLA_DK), GLA_GATE_RANK ** -0.5),
        "gla_b_gate": nrm((n_gla, GLA_HEADS * GLA_DK), 0.01),
        "gla_norm_g": gain((n_gla, GLA_DV)),
        "gla_w_out": nrm((n_gla, GLA_HEADS * GLA_DV, D), (GLA_HEADS * GLA_DV) ** -0.5),
    }


def reference(x, p, norm_mix, norm_ffn, norm_ple, norm_final, ffn_w1, ffn_w2,
              ple_w_proj, ple_w_gate, nsa_w_in, nsa_w_out, nsa_cmp_pos_k, nsa_cmp_w1_k,
              nsa_cmp_w2_k, nsa_cmp_pos_v, nsa_cmp_w1_v, nsa_cmp_w2_v, conv_w_in, conv_dw,
              conv_db, conv_ln_g, conv_ln_b, conv_w_out, gla_w_in, gla_w_gate_up,
              gla_b_gate, gla_norm_g, gla_w_out):
    h = x
    for i in range(DEPTH):
        m, j = i % N_MIXERS, i // N_MIXERS
        hn = rms_norm(h, norm_mix[i])
        if m == 0:
            y = nsa_mixer(hn, nsa_w_in[j], nsa_w_out[j], nsa_cmp_pos_k[j], nsa_cmp_w1_k[j],
                          nsa_cmp_w2_k[j], nsa_cmp_pos_v[j], nsa_cmp_w1_v[j], nsa_cmp_w2_v[j])
        elif m == 1:
            y = conv_mixer(hn, conv_w_in[j], conv_dw[j], conv_db[j], conv_ln_g[j],
                           conv_ln_b[j], conv_w_out[j])
        else:
            y = gla_mixer(hn, gla_w_in[j], gla_w_gate_up[j], gla_b_gate[j], gla_norm_g[j],
                          gla_w_out[j])
        h = h + y
        hn = rms_norm(h, norm_ffn[i])
        h = h + jnp.square(jax.nn.relu(hn @ ffn_w1[i])) @ ffn_w2[i]
        gate = jax.nn.sigmoid(rms_norm(h, norm_ple[i]) @ ple_w_gate[i])
        h = h + (p[i] @ ple_w_proj[i]) * gate
    return rms_norm(h, norm_final)
```

```python
import functools

import numpy as np
import jax
import jax.numpy as jnp
from jax import lax
from jax.experimental import pallas as pl
from jax.experimental.pallas import tpu as pltpu

BF = jnp.bfloat16
F32 = jnp.float32

NORM_EPS = 1e-6
LN_EPS = 1e-5
NEG = -1e30
LANE = 128
VMEM_LIMIT = 56 * 1024 * 1024

NSA_HEADS = 16
NSA_KV_HEADS = 4
NSA_HPG = NSA_HEADS // NSA_KV_HEADS
NSA_DH = 128
NSA_CMP_LEN = 32
NSA_CMP_STRIDE = 16
NSA_SEL_LEN = 64
NSA_SEL_TOPK = 16
NSA_SEL_LOCAL = 2
NSA_WINDOW = 512
NSA_FORCE = 1e9

CONV_WIDTH = 31
CONV_HALO = 32

GLA_HEADS = 4
GLA_DK = 256
GLA_DV = 512
GLA_GATE_RANK = 16
GLA_GATE_NORM = 16.0
GLA_CHUNK = 64

_SEL_SHIFT = NSA_SEL_LEN.bit_length() - 1
_CHUNK_SHIFT = GLA_CHUNK.bit_length() - 1

_NT = (((1,), (1,)), ((), ()))
_TN = (((0,), (0,)), ((), ()))


def _params(*sem):
    return pltpu.CompilerParams(dimension_semantics=sem, vmem_limit_bytes=VMEM_LIMIT)


def _sigmoid(x):
    return 1.0 / (1.0 + jnp.exp(-x))


def _rms_bf16(x, g):
    ms = jnp.mean(x * x, axis=-1, keepdims=True)
    return (x * lax.rsqrt(ms + NORM_EPS) * g).astype(BF)


def _proj_kernel(x_ref, g_ref, w_ref, wa_ref, o_ref, oa_ref, xn_ref):
    @pl.when(pl.program_id(1) == 0)
    def _():
        xn = _rms_bf16(x_ref[...], g_ref[...])
        xn_ref[...] = xn
        oa_ref[...] = jnp.dot(xn, wa_ref[...], preferred_element_type=F32)

    o_ref[...] = jnp.dot(xn_ref[...], w_ref[...], preferred_element_type=F32).astype(o_ref.dtype)


def _proj(x, gain, w, w_aux, *, tm, tn):
    M, K = x.shape
    N = w.shape[1]
    Na = w_aux.shape[1]
    return pl.pallas_call(
        _proj_kernel,
        grid=(M // tm, N // tn),
        in_specs=[
            pl.BlockSpec((tm, K), lambda i, j: (i, 0)),
            pl.BlockSpec((1, K), lambda i, j: (0, 0)),
            pl.BlockSpec((K, tn), lambda i, j: (0, j)),
            pl.BlockSpec((K, Na), lambda i, j: (0, 0)),
        ],
        out_specs=[
            pl.BlockSpec((tm, tn), lambda i, j: (i, j)),
            pl.BlockSpec((tm, Na), lambda i, j: (i, 0)),
        ],
        out_shape=[jax.ShapeDtypeStruct((M, N), BF), jax.ShapeDtypeStruct((M, Na), F32)],
        scratch_shapes=[pltpu.VMEM((tm, K), BF)],
        compiler_params=_params("parallel", "arbitrary"),
        name="proj",
    )(x, gain.reshape(1, K), w, w_aux)


def _glu_kernel(x_ref, g_ref, wa_ref, wg_ref, o_ref, xn_ref):
    @pl.when(pl.program_id(1) == 0)
    def _():
        xn_ref[...] = _rms_bf16(x_ref[...], g_ref[...])

    xn = xn_ref[...]
    a = jnp.dot(xn, wa_ref[...], preferred_element_type=F32)
    g = jnp.dot(xn, wg_ref[...], preferred_element_type=F32)
    o_ref[...] = a * _sigmoid(g)


def _glu_proj(x, gain, w, *, tm, tn):
    M, K = x.shape
    C = w.shape[1] // 2
    nj = C // tn
    return pl.pallas_call(
        _glu_kernel,
        grid=(M // tm, nj),
        in_specs=[
            pl.BlockSpec((tm, K), lambda i, j: (i, 0)),
            pl.BlockSpec((1, K), lambda i, j: (0, 0)),
            pl.BlockSpec((K, tn), lambda i, j: (0, j)),
            pl.BlockSpec((K, tn), lambda i, j: (0, j + nj)),
        ],
        out_specs=pl.BlockSpec((tm, tn), lambda i, j: (i, j)),
        out_shape=jax.ShapeDtypeStruct((M, C), F32),
        scratch_shapes=[pltpu.VMEM((tm, K), BF)],
        compiler_params=_params("parallel", "arbitrary"),
        name="glu_proj",
    )(x, gain.reshape(1, K), w, w)


def _outproj_kernel(a_ref, w_ref, h_ref, o_ref):
    o_ref[...] = h_ref[...] + jnp.dot(a_ref[...], w_ref[...], preferred_element_type=F32)


def _outproj(a, w, h, *, tm):
    M, K = a.shape
    N = w.shape[1]
    return pl.pallas_call(
        _outproj_kernel,
        grid=(M // tm,),
        in_specs=[
            pl.BlockSpec((tm, K), lambda i: (i, 0)),
            pl.BlockSpec((K, N), lambda i: (0, 0)),
            pl.BlockSpec((tm, N), lambda i: (i, 0)),
        ],
        out_specs=pl.BlockSpec((tm, N), lambda i: (i, 0)),
        out_shape=jax.ShapeDtypeStruct((M, N), F32),
        compiler_params=_params("parallel"),
        name="outproj",
    )(a, w, h)


def _ffn_kernel(x_ref, g_ref, w1_ref, w2_ref, o_ref, xn_ref):
    @pl.when(pl.program_id(1) == 0)
    def _():
        x = x_ref[...]
        xn_ref[...] = _rms_bf16(x, g_ref[...])
        o_ref[...] = x

    a = jnp.dot(xn_ref[...], w1_ref[...], preferred_element_type=F32)
    a = jnp.square(jnp.maximum(a, 0.0)).astype(BF)
    o_ref[...] += jnp.dot(a, w2_ref[...], preferred_element_type=F32)


def _ffn(x, gain, w1, w2, *, tm, tf):
    M, K = x.shape
    Fd = w1.shape[1]
    return pl.pallas_call(
        _ffn_kernel,
        grid=(M // tm, Fd // tf),
        in_specs=[
            pl.BlockSpec((tm, K), lambda i, j: (i, 0)),
            pl.BlockSpec((1, K), lambda i, j: (0, 0)),
            pl.BlockSpec((K, tf), lambda i, j: (0, j)),
            pl.BlockSpec((tf, K), lambda i, j: (j, 0)),
        ],
        out_specs=pl.BlockSpec((tm, K), lambda i, j: (i, 0)),
        out_shape=jax.ShapeDtypeStruct((M, K), F32),
        scratch_shapes=[pltpu.VMEM((tm, K), BF)],
        compiler_params=_params("parallel", "arbitrary"),
        name="ffn",
    )(x, gain.reshape(1, K), w1, w2)


def _ple_kernel(h_ref, p_ref, g_ref, wg_ref, wp_ref, gf_ref, o_ref, *, final):
    h = h_ref[...]
    gate = _sigmoid(jnp.dot(_rms_bf16(h, g_ref[...]), wg_ref[...], preferred_element_type=F32))
    proj = jnp.dot(p_ref[...].astype(BF), wp_ref[...], preferred_element_type=F32)
    out = h + proj * gate
    if final:
        ms = jnp.mean(out * out, axis=-1, keepdims=True)
        out = out * lax.rsqrt(ms + NORM_EPS) * gf_ref[...]
    o_ref[...] = out


def _ple(h, p, gain, w_gate, w_proj, gain_final, *, final, tm):
    M, K = h.shape
    Pd = p.shape[1]
    return pl.pallas_call(
        functools.partial(_ple_kernel, final=final),
        grid=(M // tm,),
        in_specs=[
            pl.BlockSpec((tm, K), lambda i: (i, 0)),
            pl.BlockSpec((tm, Pd), lambda i: (i, 0)),
            pl.BlockSpec((1, K), lambda i: (0, 0)),
            pl.BlockSpec((K, K), lambda i: (0, 0)),
            pl.BlockSpec((Pd, K), lambda i: (0, 0)),
            pl.BlockSpec((1, K), lambda i: (0, 0)),
        ],
        out_specs=pl.BlockSpec((tm, K), lambda i: (i, 0)),
        out_shape=jax.ShapeDtypeStruct((M, K), F32),
        compiler_params=_params("parallel"),
        name="ple",
    )(h, p, gain.reshape(1, K), w_gate, w_proj, gain_final.reshape(1, K))


def _compress_kernel(x_ref, pos_ref, w1_ref, w2_ref, o_ref):
    half = x_ref.shape[-1]
    x = x_ref[0, 0, 0]
    w1a = w1_ref[0, :half, :]
    w1b = w1_ref[0, half:, :]
    pos = pos_ref[0]
    pa = jnp.dot(pos, w1a, preferred_element_type=F32)
    pb = jnp.dot(pos, w1b, preferred_element_type=F32)
    bias = pa[0:1, :] + pb[1:2, :]
    a = jnp.dot(x, w1a, preferred_element_type=F32)
    b = jnp.dot(x, w1b, preferred_element_type=F32)
    R = a.shape[0]
    hid = a + pltpu.roll(b, R - 1, 0) + bias
    hid = hid * _sigmoid(hid)
    out = jnp.dot(hid.astype(BF), w2_ref[0], preferred_element_type=F32)
    row = lax.broadcasted_iota(jnp.int32, out.shape, 0)
    o_ref[0, 0, 0] = jnp.where(row < R - 1, out, 0.0).astype(BF)


def _compress(xc, posx, w1, w2):
    B, _, G, R, W = xc.shape
    Dh = w2.shape[-1]
    return pl.pallas_call(
        _compress_kernel,
        grid=(B, 2, G),
        in_specs=[
            pl.BlockSpec((1, 1, 1, R, W), lambda b, c, g: (b, c, g, 0, 0)),
            pl.BlockSpec((1,) + posx.shape[1:], lambda b, c, g: (c, 0, 0)),
            pl.BlockSpec((1,) + w1.shape[1:], lambda b, c, g: (c, 0, 0)),
            pl.BlockSpec((1,) + w2.shape[1:], lambda b, c, g: (c, 0, 0)),
        ],
        out_specs=pl.BlockSpec((1, 1, 1, R, Dh), lambda b, c, g: (b, c, g, 0, 0)),
        out_shape=jax.ShapeDtypeStruct((B, 2, G, R, Dh), BF),
        compiler_params=_params("parallel", "parallel", "parallel"),
        name="nsa_compress",
    )(xc, posx, w1, w2)


def _nsa_kernel(q_ref, gt_ref, kc_ref, vc_ref, ks_ref, vs_ref, kw_ref, vw_ref, ov_ref, o_ref,
                m_ref, l_ref, acc_ref, *, tq, tk, n_top):
    Dh = NSA_DH
    scale = Dh ** -0.5
    t0 = pl.program_id(2) * tq
    trow = t0 + lax.broadcasted_iota(jnp.int32, (tq, 1), 0)
    q4 = q_ref[...]
    qs = [q4[:, h * Dh:(h + 1) * Dh] for h in range(NSA_HPG)]
    gt = _sigmoid(gt_ref[...])

    kc = kc_ref[0, 0, 0]
    vc = vc_ref[0, 0, 0]
    ncp = kc.shape[0]
    n_idx = lax.broadcasted_iota(jnp.int32, (tq, ncp), 1)
    valid_c = (n_idx * NSA_CMP_STRIDE + (NSA_CMP_LEN - 1)) <= trow
    o_c = []
    psum = jnp.zeros((tq, ncp), F32)
    for h in range(NSA_HPG):
        s = lax.dot_general(qs[h], kc, _NT, preferred_element_type=F32) * scale
        s = jnp.where(valid_c, s, NEG)
        e = jnp.exp(s - jnp.max(s, axis=-1, keepdims=True))
        den = jnp.sum(e, axis=-1, keepdims=True)
        p = jnp.where(valid_c, e, 0.0) / den
        psum = psum + p
        o_c.append(jnp.dot(p.astype(BF), vc, preferred_element_type=F32))

    ov = ov_ref[...]
    p_hi = psum.astype(BF)
    p_lo = (psum - p_hi.astype(F32)).astype(BF)
    imp = (jnp.dot(p_hi, ov, preferred_element_type=F32)
           + jnp.dot(p_lo, ov, preferred_element_type=F32))
    m_idx = lax.broadcasted_iota(jnp.int32, imp.shape, 1)
    m_idx_f = m_idx.astype(F32)
    diff = jnp.right_shift(trow, _SEL_SHIFT) - m_idx
    forced = (m_idx == 0) | ((diff >= 0) & (diff < NSA_SEL_LOCAL))
    imp = jnp.where(forced, NSA_FORCE, imp)
    imp = jnp.where(diff >= 0, imp, -NSA_FORCE)
    member = jnp.zeros(imp.shape, F32)
    work = imp
    for _ in range(n_top):
        mx = jnp.max(work, axis=-1, keepdims=True)
        first = jnp.min(jnp.where(work == mx, m_idx_f, 1e6), axis=-1, keepdims=True)
        pick = m_idx_f == first
        member = jnp.where(pick, 1.0, member)
        work = jnp.where(pick, -3e38, work)
    member = jnp.where(diff >= 0, member, 0.0).astype(BF)

    m_ref[...] = jnp.full(m_ref.shape, NEG, F32)
    l_ref[...] = jnp.zeros(l_ref.shape, F32)
    acc_ref[...] = jnp.zeros(acc_ref.shape, F32)
    blk_row = lax.broadcasted_iota(jnp.int32, (LANE, tk), 0)
    key_col = lax.broadcasted_iota(jnp.int32, (LANE, tk), 1)
    kpos_col = lax.broadcasted_iota(jnp.int32, (tq, tk), 1)

    def sel_body(j, carry):
        k0 = pl.multiple_of(j * tk, tk)
        kt = ks_ref[pl.ds(k0, tk), :]
        vt = vs_ref[pl.ds(k0, tk), :]
        expand = jnp.where(jnp.right_shift(k0 + key_col, _SEL_SHIFT) == blk_row, 1.0, 0.0).astype(BF)
        mk = jnp.dot(member, expand, preferred_element_type=F32)
        ok = (mk > 0.5) & ((k0 + kpos_col) <= trow)
        bias = jnp.where(ok, 0.0, NEG)
        for h in range(NSA_HPG):
            s = lax.dot_general(qs[h], kt, _NT, preferred_element_type=F32) * scale + bias
            m_prev = m_ref[h]
            m_new = jnp.maximum(m_prev, jnp.max(s, axis=-1, keepdims=True))
            alpha = jnp.exp(m_prev - m_new)
            e = jnp.exp(s - m_new)
            l_ref[h] = alpha * l_ref[h] + jnp.sum(e, axis=-1, keepdims=True)
            acc_ref[h] = alpha * acc_ref[h] + jnp.dot(e.astype(BF), vt, preferred_element_type=F32)
            m_ref[h] = m_new
        return carry

    lax.fori_loop(0, (t0 + tq + tk - 1) // tk, sel_body, 0)

    wlen = NSA_WINDOW + tq
    ws = pl.multiple_of(jnp.maximum(t0 - NSA_WINDOW, 0), tq)
    kwt = kw_ref[pl.ds(ws, wlen), :]
    vwt = vw_ref[pl.ds(ws, wlen), :]
    kpos_w = ws + lax.broadcasted_iota(jnp.int32, (tq, wlen), 1)
    ok_w = (kpos_w <= trow) & (kpos_w > trow - NSA_WINDOW)
    bias_w = jnp.where(ok_w, 0.0, NEG)

    for h in range(NSA_HPG):
        s = lax.dot_general(qs[h], kwt, _NT, preferred_element_type=F32) * scale + bias_w
        e = jnp.exp(s - jnp.max(s, axis=-1, keepdims=True))
        den = jnp.sum(e, axis=-1, keepdims=True)
        o_w = jnp.dot(e.astype(BF), vwt, preferred_element_type=F32) / den
        o_s = acc_ref[h] / l_ref[h]
        out = (gt[:, 3 * h:3 * h + 1] * o_c[h] + gt[:, 3 * h + 1:3 * h + 2] * o_s
               + gt[:, 3 * h + 2:3 * h + 3] * o_w)
        o_ref[:, h * Dh:(h + 1) * Dh] = out.astype(BF)


def _nsa_core(P, gates, cmp_kv, ov, *, B, S, tq, tk):
    G, HPG, Dh = NSA_KV_HEADS, NSA_HPG, NSA_DH
    nq = S // tq
    ncp = cmp_kv.shape[3]
    n_top = min(NSA_SEL_TOPK, S // NSA_SEL_LEN)
    assert S >= NSA_WINDOW + tq and NSA_WINDOW % tq == 0 and S % tk == 0 and tk % NSA_SEL_LEN == 0
    assert S // NSA_SEL_LEN <= LANE
    qcols = NSA_HEADS * Dh // LANE
    kvcols = G

    def kv_spec(slab):
        off = qcols + slab * kvcols
        return pl.BlockSpec((S, Dh), lambda b, g, i: (b, off + g))

    return pl.pallas_call(
        functools.partial(_nsa_kernel, tq=tq, tk=tk, n_top=n_top),
        grid=(B, G, nq),
        in_specs=[
            pl.BlockSpec((tq, HPG * Dh), lambda b, g, i: (b * nq + i, g)),
            pl.BlockSpec((tq, LANE), lambda b, g, i: (b * nq + i, g)),
            pl.BlockSpec((1, 1, 1, ncp, Dh), lambda b, g, i: (b, 0, g, 0, 0)),
            pl.BlockSpec((1, 1, 1, ncp, Dh), lambda b, g, i: (b, 1, g, 0, 0)),
            kv_spec(2), kv_spec(3), kv_spec(4), kv_spec(5),
            pl.BlockSpec(ov.shape, lambda b, g, i: (0, 0)),
        ],
        out_specs=pl.BlockSpec((tq, HPG * Dh), lambda b, g, i: (b * nq + i, g)),
        out_shape=jax.ShapeDtypeStruct((B * S, NSA_HEADS * Dh), BF),
        scratch_shapes=[
            pltpu.VMEM((HPG, tq, 1), F32),
            pltpu.VMEM((HPG, tq, 1), F32),
            pltpu.VMEM((HPG, tq, Dh), F32),
        ],
        compiler_params=_params("parallel", "parallel", "arbitrary"),
        name="nsa_core",
    )(P, gates, cmp_kv, cmp_kv, P, P, P, P, ov)


def _overlap_matrix(S, ncp):
    n_cmp = (S - NSA_CMP_LEN) // NSA_CMP_STRIDE + 1
    n_sel = S // NSA_SEL_LEN
    cs = np.arange(n_cmp) * NSA_CMP_STRIDE
    ss = np.arange(n_sel) * NSA_SEL_LEN
    ovl = ((cs[:, None] <= ss[None, :] + NSA_SEL_LEN - 1)
           & (cs[:, None] + NSA_CMP_LEN - 1 >= ss[None, :])).astype(np.float32)
    out = np.zeros((ncp, LANE), np.float32)
    out[:n_cmp, :n_sel] = ovl
    return jnp.asarray(out, BF)


def _nsa_layer(h, gain, w_in, w_out, pos_k, w1_k, w2_k, pos_v, w1_v, w2_v, *, B, S):
    G, HPG, Dh = NSA_KV_HEADS, NSA_HPG, NSA_DH
    D = h.shape[1]
    main = NSA_HEADS * Dh + 6 * G * Dh
    w_main = w_in[:, :main].astype(BF)
    wg = w_in[:, main:].reshape(D, G, 3 * HPG)
    wg = jnp.pad(wg, ((0, 0), (0, 0), (0, LANE - 3 * HPG))).reshape(D, G * LANE).astype(BF)
    P, gates = _proj(h, gain, w_main, wg, tm=512, tn=1024)

    R = S // NSA_CMP_STRIDE
    c0 = NSA_HEADS * Dh
    xc = P[:, c0:c0 + 2 * G * Dh].reshape(B, R, NSA_CMP_STRIDE, 2, G, Dh)
    xc = xc.transpose(0, 3, 4, 1, 2, 5).reshape(B, 2, G, R, NSA_CMP_STRIDE * Dh)
    posx = jnp.stack([pos_k, pos_v]).reshape(2, 2, NSA_CMP_STRIDE * Dh)
    posx = jnp.pad(posx, ((0, 0), (0, 14), (0, 0))).astype(BF)
    w1 = jnp.stack([w1_k, w1_v]).astype(BF)
    w2 = jnp.stack([w2_k, w2_v]).astype(BF)
    cmp_kv = _compress(xc, posx, w1, w2)

    ov = _overlap_matrix(S, R)
    o = _nsa_core(P, gates, cmp_kv, ov, B=B, S=S, tq=256, tk=512)
    return _outproj(o, w_out.astype(BF), h, tm=512)


def _conv_kernel(u_ref, halo_ref, dw_ref, db_ref, lg_ref, lb_ref, w_ref, h_ref, o_ref,
                 buf_ref, y_ref, *, ts):
    C = u_ref.shape[-1]
    halo = halo_ref[...]
    buf_ref[0:CONV_HALO, :] = jnp.where(pl.program_id(1) > 0, halo, 0.0)
    buf_ref[CONV_HALO:CONV_HALO + ts, :] = u_ref[...]
    off = CONV_HALO - (CONV_WIDTH - 1)
    rs = 128

    def chan_body(c, carry):
        c0 = pl.multiple_of(c * LANE, LANE)
        for r in range(ts // rs):
            acc = jnp.zeros((rs, LANE), F32) + db_ref[:, pl.ds(c0, LANE)]
            for k in range(CONV_WIDTH):
                win = buf_ref[pl.ds(r * rs + off + k, rs), pl.ds(c0, LANE)]
                acc = acc + win * dw_ref[k:k + 1, pl.ds(c0, LANE)]
            y_ref[pl.ds(r * rs, rs), pl.ds(c0, LANE)] = acc
        return carry

    lax.fori_loop(0, C // LANE, chan_body, 0)

    y = y_ref[...]
    mu = jnp.mean(y, axis=-1, keepdims=True)
    yc = y - mu
    var = jnp.mean(yc * yc, axis=-1, keepdims=True)
    z = yc * lax.rsqrt(var + LN_EPS) * lg_ref[...] + lb_ref[...]
    z = (z * _sigmoid(z)).astype(BF)
    o_ref[...] = h_ref[...] + jnp.dot(z, w_ref[...], preferred_element_type=F32)


def _conv_tail(u, dw, db, ln_g, ln_b, w_out, h, *, B, S, ts):
    M, C = u.shape
    D = w_out.shape[1]
    ns = S // ts
    hb = ts // CONV_HALO
    dwp = jnp.pad(dw, ((0, CONV_HALO - CONV_WIDTH), (0, 0)))
    return pl.pallas_call(
        functools.partial(_conv_kernel, ts=ts),
        grid=(B, ns),
        in_specs=[
            pl.BlockSpec((ts, C), lambda b, i: (b * ns + i, 0)),
            pl.BlockSpec((CONV_HALO, C), lambda b, i: (jnp.maximum((b * ns + i) * hb - 1, 0), 0)),
            pl.BlockSpec((CONV_HALO, C), lambda b, i: (0, 0)),
            pl.BlockSpec((1, C), lambda b, i: (0, 0)),
            pl.BlockSpec((1, C), lambda b, i: (0, 0)),
            pl.BlockSpec((1, C), lambda b, i: (0, 0)),
            pl.BlockSpec((C, D), lambda b, i: (0, 0)),
            pl.BlockSpec((ts, D), lambda b, i: (b * ns + i, 0)),
        ],
        out_specs=pl.BlockSpec((ts, D), lambda b, i: (b * ns + i, 0)),
        out_shape=jax.ShapeDtypeStruct((M, D), F32),
        scratch_shapes=[pltpu.VMEM((CONV_HALO + ts, C), F32), pltpu.VMEM((ts, C), F32)],
        compiler_params=_params("parallel", "parallel"),
        name="conv_tail",
    )(u, u, dwp, db.reshape(1, C), ln_g.reshape(1, C), ln_b.reshape(1, C), w_out, h)


def _conv_layer(h, gain, w_in, dw, db, ln_g, ln_b, w_out, *, B, S):
    u = _glu_proj(h, gain, w_in.astype(BF), tm=512, tn=512)
    return _conv_tail(u, dw, db, ln_g, ln_b, w_out.astype(BF), h, B=B, S=S, ts=256)


def _gla_kernel(q_ref, k_ref, v_ref, r_ref, gz_ref, wgu_ref, bg_ref, ng_ref, o_ref, st_ref, *, ts):
    C = GLA_CHUNK

    @pl.when(pl.program_id(2) == 0)
    def _():
        st_ref[...] = jnp.zeros(st_ref.shape, F32)

    z = jnp.dot(gz_ref[...].astype(BF), wgu_ref[...], preferred_element_type=F32) + bg_ref[...]
    glog = (jnp.minimum(z, 0.0) - jnp.log(1.0 + jnp.exp(-jnp.abs(z)))) / GLA_GATE_NORM

    g1 = glog.astype(BF)
    r1 = glog - g1.astype(F32)
    g2 = r1.astype(BF)
    g3 = (r1 - g2.astype(F32)).astype(BF)
    row = lax.broadcasted_iota(jnp.int32, (ts, ts), 0)
    col = lax.broadcasted_iota(jnp.int32, (ts, ts), 1)
    same = jnp.right_shift(row, _CHUNK_SHIFT) == jnp.right_shift(col, _CHUNK_SHIFT)
    tri = jnp.where(same & (col <= row), 1.0, 0.0).astype(BF)
    full = jnp.where(same, 1.0, 0.0).astype(BF)

    def msum(mat):
        return (jnp.dot(mat, g1, preferred_element_type=F32)
                + jnp.dot(mat, g2, preferred_element_type=F32)
                + jnp.dot(mat, g3, preferred_element_type=F32))

    bcum = msum(tri)
    blast = msum(full)

    q = q_ref[...].astype(F32) * (GLA_DK ** -0.5)
    k = k_ref[...].astype(F32)
    q_dec = (q * jnp.exp(bcum)).astype(BF)
    k_intra = (k * jnp.exp(-bcum)).astype(BF)
    k_state = (k * jnp.exp(blast - bcum)).astype(BF)
    decay = jnp.exp(blast)
    causal = (lax.broadcasted_iota(jnp.int32, (C, C), 1) <= lax.broadcasted_iota(jnp.int32, (C, C), 0))

    outs = []
    for c in range(ts // C):
        sl = slice(c * C, (c + 1) * C)
        vc = v_ref[sl, :]
        a = lax.dot_general(q_dec[sl], k_intra[sl], _NT, preferred_element_type=F32)
        a = jnp.where(causal, a, 0.0).astype(BF)
        st = st_ref[...]
        o = (jnp.dot(a, vc, preferred_element_type=F32)
             + lax.dot_general(q_dec[sl], st.astype(BF), _NT, preferred_element_type=F32))
        outs.append(o)
        st_ref[...] = st * decay[c * C:c * C + 1, :] + lax.dot_general(
            vc, k_state[sl], _TN, preferred_element_type=F32)

    o = jnp.concatenate(outs, axis=0)
    o = o * lax.rsqrt(jnp.mean(o * o, axis=-1, keepdims=True) + NORM_EPS) * ng_ref[...]
    r = r_ref[...].astype(F32)
    o_ref[...] = (o * (r * _sigmoid(r))).astype(BF)


def _gla_core(P, gz, wgu, bg, ng, *, B, S, ts):
    H, dk, dv = GLA_HEADS, GLA_DK, GLA_DV
    ns = S // ts
    kq = H
    vv = 2 * H * dk // dv
    rr = vv + H
    return pl.pallas_call(
        functools.partial(_gla_kernel, ts=ts),
        grid=(B, H, ns),
        in_specs=[
            pl.BlockSpec((ts, dk), lambda b, hh, i: (b * ns + i, hh)),
            pl.BlockSpec((ts, dk), lambda b, hh, i: (b * ns + i, kq + hh)),
            pl.BlockSpec((ts, dv), lambda b, hh, i: (b * ns + i, vv + hh)),
            pl.BlockSpec((ts, dv), lambda b, hh, i: (b * ns + i, rr + hh)),
            pl.BlockSpec((ts, LANE), lambda b, hh, i: (b * ns + i, 0)),
            pl.BlockSpec((LANE, dk), lambda b, hh, i: (0, hh)),
            pl.BlockSpec((1, dk), lambda b, hh, i: (0, hh)),
            pl.BlockSpec((1, dv), lambda b, hh, i: (0, 0)),
        ],
        out_specs=pl.BlockSpec((ts, dv), lambda b, hh, i: (b * ns + i, hh)),
        out_shape=jax.ShapeDtypeStruct((B * S, H * dv), BF),
        scratch_shapes=[pltpu.VMEM((dv, dk), F32)],
        compiler_params=_params("parallel", "parallel", "arbitrary"),
        name="gla_core",
    )(P, P, P, P, gz, wgu, bg.reshape(1, H * dk), ng.reshape(1, dv))


def _gla_layer(h, gain, w_in, w_gate_up, b_gate, norm_g, w_out, *, B, S):
    H, dk, dv = GLA_HEADS, GLA_DK, GLA_DV
    main = 2 * H * dk + 2 * H * dv
    w_main = w_in[:, :main].astype(BF)
    w_gz = jnp.pad(w_in[:, main:], ((0, 0), (0, LANE - GLA_GATE_RANK))).astype(BF)
    P, gz = _proj(h, gain, w_main, w_gz, tm=512, tn=1024)
    wgu = jnp.pad(w_gate_up, ((0, LANE - GLA_GATE_RANK), (0, 0))).astype(BF)
    o = _gla_core(P, gz, wgu, b_gate, norm_g, B=B, S=S, ts=256)
    return _outproj(o, w_out.astype(BF), h, tm=512)


def kernel(x, p, norm_mix, norm_ffn, norm_ple, norm_final, ffn_w1, ffn_w2, ple_w_proj, ple_w_gate,
           nsa_w_in, nsa_w_out, nsa_cmp_pos_k, nsa_cmp_w1_k, nsa_cmp_w2_k, nsa_cmp_pos_v,
           nsa_cmp_w1_v, nsa_cmp_w2_v, conv_w_in, conv_dw, conv_db, conv_ln_g, conv_ln_b,
           conv_w_out, gla_w_in, gla_w_gate_up, gla_b_gate, gla_norm_g, gla_w_out):
    B, S, D = x.shape
    depth = p.shape[0]
    M = B * S
    h = x.reshape(M, D)
    for i in range(depth):
        m, j = i % 3, i // 3
        if m == 0:
            h = _nsa_layer(h, norm_mix[i], nsa_w_in[j], nsa_w_out[j], nsa_cmp_pos_k[j],
                           nsa_cmp_w1_k[j], nsa_cmp_w2_k[j], nsa_cmp_pos_v[j], nsa_cmp_w1_v[j],
                           nsa_cmp_w2_v[j], B=B, S=S)
        elif m == 1:
            h = _conv_layer(h, norm_mix[i], conv_w_in[j], conv_dw[j], conv_db[j], conv_ln_g[j],
                            conv_ln_b[j], conv_w_out[j], B=B, S=S)
        else:
            h = _gla_layer(h, norm_mix[i], gla_w_in[j], gla_w_gate_up[j], gla_b_gate[j],
                           gla_norm_g[j], gla_w_out[j], B=B, S=S)
        h = _ffn(h, norm_ffn[i], ffn_w1[i].astype(BF), ffn_w2[i].astype(BF), tm=512, tf=1024)
        h = _ple(h, p[i].reshape(M, p.shape[-1]), norm_ple[i], ple_w_gate[i].astype(BF),
                 ple_w_proj[i].astype(BF), norm_final, final=(i == depth - 1), tm=256)
    return h.reshape(B, S, D)
```

```python
import functools

import numpy as np
import jax
import jax.numpy as jnp
from jax import lax
from jax.experimental import pallas as pl
from jax.experimental.pallas import tpu as pltpu

BF = jnp.bfloat16
F32 = jnp.float32

NORM_EPS = 1e-6
LN_EPS = 1e-5
NEG = -1e30
LANE = 128
VMEM_LIMIT = 56 * 1024 * 1024

NSA_HEADS = 16
NSA_KV_HEADS = 4
NSA_HPG = NSA_HEADS // NSA_KV_HEADS
NSA_DH = 128
NSA_CMP_LEN = 32
NSA_CMP_STRIDE = 16
NSA_SEL_LEN = 64
NSA_SEL_TOPK = 16
NSA_SEL_LOCAL = 2
NSA_WINDOW = 512
NSA_FORCE = 1e9

CONV_WIDTH = 31
CONV_HALO = 32

GLA_HEADS = 4
GLA_DK = 256
GLA_DV = 512
GLA_GATE_RANK = 16
GLA_GATE_NORM = 16.0
GLA_CHUNK = 64

_LOG2E = 1.4426950408889634
_SEL_SHIFT = NSA_SEL_LEN.bit_length() - 1
_CHUNK_SHIFT = GLA_CHUNK.bit_length() - 1

_NT = (((1,), (1,)), ((), ()))
_TN = (((0,), (0,)), ((), ()))


def _params(*sem):
    return pltpu.CompilerParams(dimension_semantics=sem, vmem_limit_bytes=VMEM_LIMIT)


def _sigmoid(x):
    return 1.0 / (1.0 + jnp.exp(-x))


def _rms_bf16(x, g):
    ms = jnp.mean(x * x, axis=-1, keepdims=True)
    return (x * lax.rsqrt(ms + NORM_EPS) * g).astype(BF)


def _proj_kernel(x_ref, g_ref, w_ref, wa_ref, o_ref, oa_ref, xn_ref):
    @pl.when(pl.program_id(1) == 0)
    def _():
        xn = _rms_bf16(x_ref[...], g_ref[...])
        xn_ref[...] = xn
        oa_ref[...] = jnp.dot(xn, wa_ref[...], preferred_element_type=F32)

    o_ref[...] = jnp.dot(xn_ref[...], w_ref[...], preferred_element_type=F32).astype(o_ref.dtype)


def _proj(x, gain, w, w_aux, *, tm, tn):
    M, K = x.shape
    N = w.shape[1]
    Na = w_aux.shape[1]
    return pl.pallas_call(
        _proj_kernel,
        grid=(M // tm, N // tn),
        in_specs=[
            pl.BlockSpec((tm, K), lambda i, j: (i, 0)),
            pl.BlockSpec((1, K), lambda i, j: (0, 0)),
            pl.BlockSpec((K, tn), lambda i, j: (0, j)),
            pl.BlockSpec((K, Na), lambda i, j: (0, 0)),
        ],
        out_specs=[
            pl.BlockSpec((tm, tn), lambda i, j: (i, j)),
            pl.BlockSpec((tm, Na), lambda i, j: (i, 0)),
        ],
        out_shape=[jax.ShapeDtypeStruct((M, N), BF), jax.ShapeDtypeStruct((M, Na), F32)],
        scratch_shapes=[pltpu.VMEM((tm, K), BF)],
        compiler_params=_params("parallel", "arbitrary"),
        name="proj",
    )(x, gain.reshape(1, K), w, w_aux)


def _glu_kernel(x_ref, g_ref, wa_ref, wg_ref, o_ref, xn_ref):
    @pl.when(pl.program_id(1) == 0)
    def _():
        xn_ref[...] = _rms_bf16(x_ref[...], g_ref[...])

    xn = xn_ref[...]
    a = jnp.dot(xn, wa_ref[...], preferred_element_type=F32)
    g = jnp.dot(xn, wg_ref[...], preferred_element_type=F32)
    o_ref[...] = a * _sigmoid(g)


def _glu_proj(x, gain, w, *, tm, tn):
    M, K = x.shape
    C = w.shape[1] // 2
    nj = C // tn
    return pl.pallas_call(
        _glu_kernel,
        grid=(M // tm, nj),
        in_specs=[
            pl.BlockSpec((tm, K), lambda i, j: (i, 0)),
            pl.BlockSpec((1, K), lambda i, j: (0, 0)),
            pl.BlockSpec((K, tn), lambda i, j: (0, j)),
            pl.BlockSpec((K, tn), lambda i, j: (0, j + nj)),
        ],
        out_specs=pl.BlockSpec((tm, tn), lambda i, j: (i, j)),
        out_shape=jax.ShapeDtypeStruct((M, C), F32),
        scratch_shapes=[pltpu.VMEM((tm, K), BF)],
        compiler_params=_params("parallel", "arbitrary"),
        name="glu_proj",
    )(x, gain.reshape(1, K), w, w)


def _outproj_kernel(a_ref, w_ref, h_ref, o_ref):
    o_ref[...] = h_ref[...] + jnp.dot(a_ref[...], w_ref[...], preferred_element_type=F32)


def _outproj(a, w, h, *, tm):
    M, K = a.shape
    N = w.shape[1]
    return pl.pallas_call(
        _outproj_kernel,
        grid=(M // tm,),
        in_specs=[
            pl.BlockSpec((tm, K), lambda i: (i, 0)),
            pl.BlockSpec((K, N), lambda i: (0, 0)),
            pl.BlockSpec((tm, N), lambda i: (i, 0)),
        ],
        out_specs=pl.BlockSpec((tm, N), lambda i: (i, 0)),
        out_shape=jax.ShapeDtypeStruct((M, N), F32),
        compiler_params=_params("parallel"),
        name="outproj",
    )(a, w, h)


def _ffn_kernel(x_ref, g_ref, w1_ref, w2_ref, o_ref, xn_ref):
    @pl.when(pl.program_id(1) == 0)
    def _():
        x = x_ref[...]
        xn_ref[...] = _rms_bf16(x, g_ref[...])
        o_ref[...] = x

    a = jnp.dot(xn_ref[...], w1_ref[...], preferred_element_type=F32)
    a = jnp.square(jnp.maximum(a, 0.0)).astype(BF)
    o_ref[...] += jnp.dot(a, w2_ref[...], preferred_element_type=F32)


def _ffn(x, gain, w1, w2, layer, *, tm, tf):
    M, K = x.shape
    Fd = w1.shape[2]
    return pl.pallas_call(
        _ffn_kernel,
        grid=(M // tm, Fd // tf),
        in_specs=[
            pl.BlockSpec((tm, K), lambda i, j: (i, 0)),
            pl.BlockSpec((1, K), lambda i, j: (0, 0)),
            pl.BlockSpec((None, K, tf), lambda i, j: (layer, 0, j)),
            pl.BlockSpec((None, tf, K), lambda i, j: (layer, j, 0)),
        ],
        out_specs=pl.BlockSpec((tm, K), lambda i, j: (i, 0)),
        out_shape=jax.ShapeDtypeStruct((M, K), F32),
        scratch_shapes=[pltpu.VMEM((tm, K), BF)],
        compiler_params=_params("parallel", "arbitrary"),
        name="ffn",
    )(x, gain.reshape(1, K), w1, w2)


def _ple_kernel(h_ref, p_ref, g_ref, wg_ref, wp_ref, gf_ref, o_ref, *, final):
    h = h_ref[...]
    gate = _sigmoid(jnp.dot(_rms_bf16(h, g_ref[...]), wg_ref[...], preferred_element_type=F32))
    proj = jnp.dot(p_ref[...].astype(BF), wp_ref[...], preferred_element_type=F32)
    out = h + proj * gate
    if final:
        ms = jnp.mean(out * out, axis=-1, keepdims=True)
        out = out * lax.rsqrt(ms + NORM_EPS) * gf_ref[...]
    o_ref[...] = out


def _ple(h, p, gain, w_gate, w_proj, layer, gain_final, *, final, tm):
    M, K = h.shape
    Pd = p.shape[2]
    return pl.pallas_call(
        functools.partial(_ple_kernel, final=final),
        grid=(M // tm,),
        in_specs=[
            pl.BlockSpec((tm, K), lambda i: (i, 0)),
            pl.BlockSpec((None, tm, Pd), lambda i: (layer, i, 0)),
            pl.BlockSpec((1, K), lambda i: (0, 0)),
            pl.BlockSpec((None, K, K), lambda i: (layer, 0, 0)),
            pl.BlockSpec((None, Pd, K), lambda i: (layer, 0, 0)),
            pl.BlockSpec((1, K), lambda i: (0, 0)),
        ],
        out_specs=pl.BlockSpec((tm, K), lambda i: (i, 0)),
        out_shape=jax.ShapeDtypeStruct((M, K), F32),
        compiler_params=_params("parallel"),
        name="ple",
    )(h, p, gain.reshape(1, K), w_gate, w_proj, gain_final.reshape(1, K))


def _compress_kernel(x_ref, pos_ref, w1_ref, w2_ref, o_ref):
    Dh = x_ref.shape[-1]
    R = x_ref.shape[0] // NSA_CMP_STRIDE
    nl = NSA_CMP_STRIDE
    a = jnp.zeros((R, w1_ref.shape[-1]), F32)
    b = jnp.zeros((R, w1_ref.shape[-1]), F32)
    for l in range(nl):
        xl = x_ref[pl.ds(l, R, stride=nl), :]
        a = a + jnp.dot((xl + pos_ref[0, l:l + 1, :]).astype(BF),
                        w1_ref[0, l * Dh:(l + 1) * Dh, :], preferred_element_type=F32)
        b = b + jnp.dot((xl + pos_ref[0, nl + l:nl + l + 1, :]).astype(BF),
                        w1_ref[0, (nl + l) * Dh:(nl + l + 1) * Dh, :], preferred_element_type=F32)
    hid = a + pltpu.roll(b, R - 1, 0)
    hid = hid * _sigmoid(hid)
    out = jnp.dot(hid.astype(BF), w2_ref[0], preferred_element_type=F32)
    row = lax.broadcasted_iota(jnp.int32, out.shape, 0)
    o_ref[0, 0, 0] = jnp.where(row < R - 1, out, 0.0).astype(BF)


def _compress(aux, pos, w1, w2, *, B, S, col0):
    G, Dh = NSA_KV_HEADS, NSA_DH
    R = S // NSA_CMP_STRIDE
    return pl.pallas_call(
        _compress_kernel,
        grid=(B, 2, G),
        in_specs=[
            pl.BlockSpec((S, Dh), lambda b, c, g: (b, col0 + c * G + g)),
            pl.BlockSpec((1,) + pos.shape[1:], lambda b, c, g: (c, 0, 0)),
            pl.BlockSpec((1,) + w1.shape[1:], lambda b, c, g: (c, 0, 0)),
            pl.BlockSpec((1,) + w2.shape[1:], lambda b, c, g: (c, 0, 0)),
        ],
        out_specs=pl.BlockSpec((1, 1, 1, R, Dh), lambda b, c, g: (b, c, g, 0, 0)),
        out_shape=jax.ShapeDtypeStruct((B, 2, G, R, Dh), BF),
        compiler_params=_params("parallel", "parallel", "parallel"),
        name="nsa_compress",
    )(aux, pos, w1, w2)


def _nsa_kernel(q_ref, gt_ref, kc_ref, vc_ref, ks_ref, vs_ref, kw_ref, vw_ref, ovt_ref, o_ref,
                vst_ref, vwt_ref, bias_ref, m_ref, l_ref, acc_ref, *, tq, tk, n_top):
    Dh = NSA_DH
    i = pl.program_id(2)
    t0 = i * tq
    S = ks_ref.shape[0]
    tc = 512

    @pl.when(i == 0)
    def _():
        for c in range(S // tc):
            vst_ref[:, c * tc:(c + 1) * tc] = vs_ref[c * tc:(c + 1) * tc, :].astype(F32).T.astype(BF)
            vwt_ref[:, c * tc:(c + 1) * tc] = vw_ref[c * tc:(c + 1) * tc, :].astype(F32).T.astype(BF)

    H = NSA_HPG
    tcol = t0 + lax.broadcasted_iota(jnp.int32, (1, tq), 1)
    qst = jnp.concatenate([q_ref[:, h * Dh:(h + 1) * Dh] for h in range(H)], axis=0)
    gtT = _sigmoid(gt_ref[...]).T

    def heads(x):
        return jnp.concatenate([x] * H, axis=1)

    kc = kc_ref[0, 0, 0]
    vcT = vc_ref[0, 0, 0].astype(F32).T.astype(BF)
    ncp = kc.shape[0]
    n_row = lax.broadcasted_iota(jnp.int32, (ncp, tq), 0)
    valid_c = heads((n_row * NSA_CMP_STRIDE + (NSA_CMP_LEN - 1)) <= tcol)
    s = lax.dot_general(kc, qst, _NT, preferred_element_type=F32)
    s = jnp.where(valid_c, s, NEG)
    e = jnp.exp2(s - jnp.max(s, axis=0, keepdims=True))
    den = jnp.sum(e, axis=0, keepdims=True)
    p = jnp.where(valid_c, e, 0.0) * (1.0 / den)
    o_c = jnp.dot(vcT, p.astype(BF), preferred_element_type=F32)
    psum = p[:, 0:tq]
    for h in range(1, H):
        psum = psum + p[:, h * tq:(h + 1) * tq]

    ovt = ovt_ref[...]
    p_hi = psum.astype(BF)
    p_lo = (psum - p_hi.astype(F32)).astype(BF)
    imp = (jnp.dot(ovt, p_hi, preferred_element_type=F32)
           + jnp.dot(ovt, p_lo, preferred_element_type=F32))
    n_sel = imp.shape[0]
    m_idx = lax.broadcasted_iota(jnp.int32, imp.shape, 0)
    m_idx_f = m_idx.astype(F32)
    diff = jnp.right_shift(tcol, _SEL_SHIFT) - m_idx
    forced = (m_idx == 0) | ((diff >= 0) & (diff < NSA_SEL_LOCAL))
    imp = jnp.where(forced, NSA_FORCE, imp)
    imp = jnp.where(diff >= 0, imp, -NSA_FORCE)
    member = jnp.zeros(imp.shape, F32)
    work = imp
    for _ in range(n_top):
        mx = jnp.max(work, axis=0, keepdims=True)
        first = jnp.min(jnp.where(work == mx, m_idx_f, 1e6), axis=0, keepdims=True)
        pick = m_idx_f == first
        member = jnp.where(pick, 1.0, member)
        work = jnp.where(pick, -3e38, work)
    sel_bias = jnp.where((member > 0.5) & (diff >= 0), 0.0, NEG)
    for mb in range(n_sel):
        bias_ref[mb] = sel_bias[mb:mb + 1, :]

    m_ref[...] = jnp.full(m_ref.shape, NEG, F32)
    l_ref[...] = jnp.zeros(l_ref.shape, F32)
    acc_ref[...] = jnp.zeros(acc_ref.shape, F32)
    key_row = lax.broadcasted_iota(jnp.int32, (tk, tq), 0)
    bpt = tk // NSA_SEL_LEN

    def sel_body(j, carry):
        k0 = pl.multiple_of(j * tk, tk)
        kt = ks_ref[pl.ds(k0, tk), :]
        vtT = vst_ref[:, pl.ds(k0, tk)]
        rows = [jnp.broadcast_to(bias_ref[j * bpt + jj], (NSA_SEL_LEN, tq)) for jj in range(bpt)]
        bias = jnp.concatenate(rows, axis=0)
        bias = jnp.where((k0 + key_row) <= tcol, bias, NEG)
        s = lax.dot_general(kt, qst, _NT, preferred_element_type=F32) + heads(bias)
        m_prev = m_ref[...]
        m_new = jnp.maximum(m_prev, jnp.max(s, axis=0, keepdims=True))
        alpha = jnp.exp2(m_prev - m_new)
        e = jnp.exp2(s - m_new)
        l_ref[...] = alpha * l_ref[...] + jnp.sum(e, axis=0, keepdims=True)
        acc_ref[...] = alpha * acc_ref[...] + jnp.dot(vtT, e.astype(BF), preferred_element_type=F32)
        m_ref[...] = m_new
        return carry

    lax.fori_loop(0, (t0 + tq + tk - 1) // tk, sel_body, 0)

    wlen = NSA_WINDOW + tq
    ws = pl.multiple_of(jnp.maximum(t0 - NSA_WINDOW, 0), tq)
    kwt = kw_ref[pl.ds(ws, wlen), :]
    vwT = vwt_ref[:, pl.ds(ws, wlen)]
    kpos_w = ws + lax.broadcasted_iota(jnp.int32, (wlen, tq), 0)
    ok_w = (kpos_w <= tcol) & (kpos_w > tcol - NSA_WINDOW)
    bias_w = jnp.where(ok_w, 0.0, NEG)
    s = lax.dot_general(kwt, qst, _NT, preferred_element_type=F32) + heads(bias_w)
    e = jnp.exp2(s - jnp.max(s, axis=0, keepdims=True))
    den = jnp.sum(e, axis=0, keepdims=True)
    o_w = jnp.dot(vwT, e.astype(BF), preferred_element_type=F32) * (1.0 / den)
    o_s = acc_ref[...] * (1.0 / l_ref[...])

    for h in range(H):
        sl = slice(h * tq, (h + 1) * tq)
        out = (gtT[3 * h:3 * h + 1, :] * o_c[:, sl] + gtT[3 * h + 1:3 * h + 2, :] * o_s[:, sl]
               + gtT[3 * h + 2:3 * h + 3, :] * o_w[:, sl])
        o_ref[:, h * Dh:(h + 1) * Dh] = out.T.astype(BF)


def _nsa_core(P, gates, cmp_kv, ovt, *, B, S, tq, tk):
    G, HPG, Dh = NSA_KV_HEADS, NSA_HPG, NSA_DH
    nq = S // tq
    ncp = cmp_kv.shape[3]
    n_sel = S // NSA_SEL_LEN
    n_top = min(NSA_SEL_TOPK, n_sel)
    assert S >= NSA_WINDOW + tq and NSA_WINDOW % tq == 0 and S % tk == 0 and tk % NSA_SEL_LEN == 0
    assert S % 512 == 0
    qcols = NSA_HEADS * Dh // LANE
    kvcols = G

    def kv_spec(slab):
        off = qcols + slab * kvcols
        return pl.BlockSpec((S, Dh), lambda b, g, i: (b, off + g))

    return pl.pallas_call(
        functools.partial(_nsa_kernel, tq=tq, tk=tk, n_top=n_top),
        grid=(B, G, nq),
        in_specs=[
            pl.BlockSpec((tq, HPG * Dh), lambda b, g, i: (b * nq + i, g)),
            pl.BlockSpec((tq, LANE), lambda b, g, i: (b * nq + i, g)),
            pl.BlockSpec((1, 1, 1, ncp, Dh), lambda b, g, i: (b, 0, g, 0, 0)),
            pl.BlockSpec((1, 1, 1, ncp, Dh), lambda b, g, i: (b, 1, g, 0, 0)),
            kv_spec(0), kv_spec(1), kv_spec(2), kv_spec(3),
            pl.BlockSpec(ovt.shape, lambda b, g, i: (0, 0)),
        ],
        out_specs=pl.BlockSpec((tq, HPG * Dh), lambda b, g, i: (b * nq + i, g)),
        out_shape=jax.ShapeDtypeStruct((B * S, NSA_HEADS * Dh), BF),
        scratch_shapes=[
            pltpu.VMEM((Dh, S), BF),
            pltpu.VMEM((Dh, S), BF),
            pltpu.VMEM((n_sel, 1, tq), F32),
            pltpu.VMEM((1, HPG * tq), F32),
            pltpu.VMEM((1, HPG * tq), F32),
            pltpu.VMEM((Dh, HPG * tq), F32),
        ],
        compiler_params=_params("parallel", "parallel", "arbitrary"),
        name="nsa_core",
    )(P, gates, cmp_kv, cmp_kv, P, P, P, P, ovt)


def _overlap_matrix_t(S, ncp):
    n_cmp = (S - NSA_CMP_LEN) // NSA_CMP_STRIDE + 1
    n_sel = S // NSA_SEL_LEN
    cs = np.arange(n_cmp) * NSA_CMP_STRIDE
    ss = np.arange(n_sel) * NSA_SEL_LEN
    ovl = ((cs[:, None] <= ss[None, :] + NSA_SEL_LEN - 1)
           & (cs[:, None] + NSA_CMP_LEN - 1 >= ss[None, :])).astype(np.float32)
    out = np.zeros((n_sel, ncp), np.float32)
    out[:, :n_cmp] = ovl.T
    return jnp.asarray(out, BF)


def _nsa_layer(h, gain, w_in, w_out, pos_k, w1_k, w2_k, pos_v, w1_v, w2_v, *, B, S):
    G, HPG, Dh = NSA_KV_HEADS, NSA_HPG, NSA_DH
    D = h.shape[1]
    qw, kvw = NSA_HEADS * Dh, G * Dh
    main = qw + 6 * kvw
    w_main = jnp.concatenate([w_in[:, :qw] * (Dh ** -0.5 * _LOG2E), w_in[:, qw + 2 * kvw:main]],
                             axis=1).astype(BF)
    wg = w_in[:, main:].reshape(D, G, 3 * HPG)
    wg = jnp.pad(wg, ((0, 0), (0, 0), (0, LANE - 3 * HPG))).reshape(D, G * LANE)
    w_aux = jnp.concatenate([wg, w_in[:, qw:qw + 2 * kvw]], axis=1).astype(BF)
    P, aux = _proj(h, gain, w_main, w_aux, tm=512, tn=1024)

    pos = jnp.stack([pos_k, pos_v])
    w1 = jnp.stack([w1_k, w1_v]).astype(BF)
    w2 = jnp.stack([w2_k, w2_v]).astype(BF)
    cmp_kv = _compress(aux, pos, w1, w2, B=B, S=S, col0=G)

    ovt = _overlap_matrix_t(S, S // NSA_CMP_STRIDE)
    o = _nsa_core(P, aux, cmp_kv, ovt, B=B, S=S, tq=256, tk=512)
    return _outproj(o, w_out.astype(BF), h, tm=512)


def _conv_kernel(u_ref, halo_ref, dw_ref, db_ref, lg_ref, lb_ref, w_ref, h_ref, o_ref,
                 buf_ref, sh_ref, y_ref, *, ts):
    C = u_ref.shape[-1]
    halo = halo_ref[...]
    buf_ref[0:CONV_HALO, :] = jnp.where(pl.program_id(1) > 0, halo, 0.0)
    buf_ref[CONV_HALO:CONV_HALO + ts, :] = u_ref[...]
    off = CONV_HALO - (CONV_WIDTH - 1)
    rs = 128
    sub = 8
    span = ts + CONV_HALO - sub

    def chan_body(c, carry):
        c0 = pl.multiple_of(c * LANE, LANE)
        for r in range(1, sub):
            sh_ref[r, 0:span, :] = buf_ref[pl.ds(r, span), pl.ds(c0, LANE)]
        for rr in range(ts // rs):
            acc = jnp.zeros((rs, LANE), F32) + db_ref[:, pl.ds(c0, LANE)]
            for k in range(CONV_WIDTH):
                r = (off + k) % sub
                base = rr * rs + off + k - r
                if r == 0:
                    win = buf_ref[pl.ds(base, rs), pl.ds(c0, LANE)]
                else:
                    win = sh_ref[r, pl.ds(base, rs), :]
                acc = acc + win * dw_ref[k:k + 1, pl.ds(c0, LANE)]
            y_ref[pl.ds(rr * rs, rs), pl.ds(c0, LANE)] = acc
        return carry

    lax.fori_loop(0, C // LANE, chan_body, 0)

    y = y_ref[...]
    mu = jnp.mean(y, axis=-1, keepdims=True)
    yc = y - mu
    var = jnp.mean(yc * yc, axis=-1, keepdims=True)
    z = yc * lax.rsqrt(var + LN_EPS) * lg_ref[...] + lb_ref[...]
    z = (z * _sigmoid(z)).astype(BF)
    o_ref[...] = h_ref[...] + jnp.dot(z, w_ref[...], preferred_element_type=F32)


def _conv_tail(u, dw, db, ln_g, ln_b, w_out, h, *, B, S, ts):
    M, C = u.shape
    D = w_out.shape[1]
    ns = S // ts
    hb = ts // CONV_HALO
    dwp = jnp.pad(dw, ((0, CONV_HALO - CONV_WIDTH), (0, 0)))
    return pl.pallas_call(
        functools.partial(_conv_kernel, ts=ts),
        grid=(B, ns),
        in_specs=[
            pl.BlockSpec((ts, C), lambda b, i: (b * ns + i, 0)),
            pl.BlockSpec((CONV_HALO, C), lambda b, i: (jnp.maximum((b * ns + i) * hb - 1, 0), 0)),
            pl.BlockSpec((CONV_HALO, C), lambda b, i: (0, 0)),
            pl.BlockSpec((1, C), lambda b, i: (0, 0)),
            pl.BlockSpec((1, C), lambda b, i: (0, 0)),
            pl.BlockSpec((1, C), lambda b, i: (0, 0)),
            pl.BlockSpec((C, D), lambda b, i: (0, 0)),
            pl.BlockSpec((ts, D), lambda b, i: (b * ns + i, 0)),
        ],
        out_specs=pl.BlockSpec((ts, D), lambda b, i: (b * ns + i, 0)),
        out_shape=jax.ShapeDtypeStruct((M, D), F32),
        scratch_shapes=[pltpu.VMEM((CONV_HALO + ts, C), F32), pltpu.VMEM((8, CONV_HALO + ts, LANE), F32),
                        pltpu.VMEM((ts, C), F32)],
        compiler_params=_params("parallel", "parallel"),
        name="conv_tail",
    )(u, u, dwp, db.reshape(1, C), ln_g.reshape(1, C), ln_b.reshape(1, C), w_out, h)


def _conv_layer(h, gain, w_in, dw, db, ln_g, ln_b, w_out, *, B, S):
    u = _glu_proj(h, gain, w_in.astype(BF), tm=512, tn=512)
    return _conv_tail(u, dw, db, ln_g, ln_b, w_out.astype(BF), h, B=B, S=S, ts=256)


def _gla_kernel(q_ref, k_ref, v_ref, r_ref, gz_ref, wgu_ref, bg_ref, ng_ref, o_ref, st_ref, *, ts):
    C = GLA_CHUNK

    @pl.when(pl.program_id(2) == 0)
    def _():
        st_ref[...] = jnp.zeros(st_ref.shape, F32)

    z = jnp.dot(gz_ref[...].astype(BF), wgu_ref[...], preferred_element_type=F32) + bg_ref[...]
    glog = (jnp.minimum(z, 0.0) - jnp.log(1.0 + jnp.exp(-jnp.abs(z)))) / GLA_GATE_NORM

    g1 = glog.astype(BF)
    r1 = glog - g1.astype(F32)
    g2 = r1.astype(BF)
    g3 = (r1 - g2.astype(F32)).astype(BF)
    row = lax.broadcasted_iota(jnp.int32, (ts, ts), 0)
    col = lax.broadcasted_iota(jnp.int32, (ts, ts), 1)
    same = jnp.right_shift(row, _CHUNK_SHIFT) == jnp.right_shift(col, _CHUNK_SHIFT)
    tri = jnp.where(same & (col <= row), 1.0, 0.0).astype(BF)
    full = jnp.where(same, 1.0, 0.0).astype(BF)

    def msum(mat):
        return (jnp.dot(mat, g1, preferred_element_type=F32)
                + jnp.dot(mat, g2, preferred_element_type=F32)
                + jnp.dot(mat, g3, preferred_element_type=F32))

    bcum = msum(tri)
    blast = msum(full)

    q = q_ref[...].astype(F32) * (GLA_DK ** -0.5)
    k = k_ref[...].astype(F32)
    q_dec = (q * jnp.exp(bcum)).astype(BF)
    k_intra = (k * jnp.exp(-bcum)).astype(BF)
    k_state = (k * jnp.exp(blast - bcum)).astype(BF)
    decay = jnp.exp(blast)
    causal = (lax.broadcasted_iota(jnp.int32, (C, C), 1) <= lax.broadcasted_iota(jnp.int32, (C, C), 0))

    outs = []
    for c in range(ts // C):
        sl = slice(c * C, (c + 1) * C)
        vc = v_ref[sl, :]
        a = lax.dot_general(q_dec[sl], k_intra[sl], _NT, preferred_element_type=F32)
        a = jnp.where(causal, a, 0.0).astype(BF)
        st = st_ref[...]
        o = (jnp.dot(a, vc, preferred_element_type=F32)
             + lax.dot_general(q_dec[sl], st.astype(BF), _NT, preferred_element_type=F32))
        outs.append(o)
        st_ref[...] = st * decay[c * C:c * C + 1, :] + lax.dot_general(
            vc, k_state[sl], _TN, preferred_element_type=F32)

    o = jnp.concatenate(outs, axis=0)
    o = o * lax.rsqrt(jnp.mean(o * o, axis=-1, keepdims=True) + NORM_EPS) * ng_ref[...]
    r = r_ref[...].astype(F32)
    o_ref[...] = (o * (r * _sigmoid(r))).astype(BF)


def _gla_core(P, gz, wgu, bg, ng, *, B, S, ts):
    H, dk, dv = GLA_HEADS, GLA_DK, GLA_DV
    ns = S // ts
    kq = H
    vv = 2 * H * dk // dv
    rr = vv + H
    return pl.pallas_call(
        functools.partial(_gla_kernel, ts=ts),
        grid=(B, H, ns),
        in_specs=[
            pl.BlockSpec((ts, dk), lambda b, hh, i: (b * ns + i, hh)),
            pl.BlockSpec((ts, dk), lambda b, hh, i: (b * ns + i, kq + hh)),
            pl.BlockSpec((ts, dv), lambda b, hh, i: (b * ns + i, vv + hh)),
            pl.BlockSpec((ts, dv), lambda b, hh, i: (b * ns + i, rr + hh)),
            pl.BlockSpec((ts, LANE), lambda b, hh, i: (b * ns + i, 0)),
            pl.BlockSpec((LANE, dk), lambda b, hh, i: (0, hh)),
            pl.BlockSpec((1, dk), lambda b, hh, i: (0, hh)),
            pl.BlockSpec((1, dv), lambda b, hh, i: (0, 0)),
        ],
        out_specs=pl.BlockSpec((ts, dv), lambda b, hh, i: (b * ns + i, hh)),
        out_shape=jax.ShapeDtypeStruct((B * S, H * dv), BF),
        scratch_shapes=[pltpu.VMEM((dv, dk), F32)],
        compiler_params=_params("parallel", "parallel", "arbitrary"),
        name="gla_core",
    )(P, P, P, P, gz, wgu, bg.reshape(1, H * dk), ng.reshape(1, dv))


def _gla_layer(h, gain, w_in, w_gate_up, b_gate, norm_g, w_out, *, B, S):
    H, dk, dv = GLA_HEADS, GLA_DK, GLA_DV
    main = 2 * H * dk + 2 * H * dv
    w_main = w_in[:, :main].astype(BF)
    w_gz = jnp.pad(w_in[:, main:], ((0, 0), (0, LANE - GLA_GATE_RANK))).astype(BF)
    P, gz = _proj(h, gain, w_main, w_gz, tm=512, tn=1024)
    wgu = jnp.pad(w_gate_up, ((0, LANE - GLA_GATE_RANK), (0, 0))).astype(BF)
    o = _gla_core(P, gz, wgu, b_gate, norm_g, B=B, S=S, ts=256)
    return _outproj(o, w_out.astype(BF), h, tm=512)


def kernel(x, p, norm_mix, norm_ffn, norm_ple, norm_final, ffn_w1, ffn_w2, ple_w_proj, ple_w_gate,
           nsa_w_in, nsa_w_out, nsa_cmp_pos_k, nsa_cmp_w1_k, nsa_cmp_w2_k, nsa_cmp_pos_v,
           nsa_cmp_w1_v, nsa_cmp_w2_v, conv_w_in, conv_dw, conv_db, conv_ln_g, conv_ln_b,
           conv_w_out, gla_w_in, gla_w_gate_up, gla_b_gate, gla_norm_g, gla_w_out):
    B, S, D = x.shape
    depth = p.shape[0]
    M = B * S
    h = x.reshape(M, D)
    p2 = p.reshape(depth, M, p.shape[-1])
    ffn_w1, ffn_w2 = ffn_w1.astype(BF), ffn_w2.astype(BF)
    ple_w_gate, ple_w_proj = ple_w_gate.astype(BF), ple_w_proj.astype(BF)
    for i in range(depth):
        m, j = i % 3, i // 3
        if m == 0:
            h = _nsa_layer(h, norm_mix[i], nsa_w_in[j], nsa_w_out[j], nsa_cmp_pos_k[j],
                           nsa_cmp_w1_k[j], nsa_cmp_w2_k[j], nsa_cmp_pos_v[j], nsa_cmp_w1_v[j],
                           nsa_cmp_w2_v[j], B=B, S=S)
        elif m == 1:
            h = _conv_layer(h, norm_mix[i], conv_w_in[j], conv_dw[j], conv_db[j], conv_ln_g[j],
                            conv_ln_b[j], conv_w_out[j], B=B, S=S)
        else:
            h = _gla_layer(h, norm_mix[i], gla_w_in[j], gla_w_gate_up[j], gla_b_gate[j],
                           gla_norm_g[j], gla_w_out[j], B=B, S=S)
        h = _ffn(h, norm_ffn[i], ffn_w1, ffn_w2, i, tm=512, tf=1024)
        h = _ple(h, p2, norm_ple[i], ple_w_gate, ple_w_proj, i, norm_final,
                 final=(i == depth - 1), tm=256)
    return h.reshape(B, S, D)
```

```python
import functools

import numpy as np
import jax
import jax.numpy as jnp
from jax import lax
from jax.experimental import pallas as pl
from jax.experimental.pallas import tpu as pltpu

BF = jnp.bfloat16
F32 = jnp.float32

NORM_EPS = 1e-6
LN_EPS = 1e-5
NEG = -1e30
LANE = 128
VMEM_LIMIT = 56 * 1024 * 1024

NSA_HEADS = 16
NSA_KV_HEADS = 4
NSA_HPG = NSA_HEADS // NSA_KV_HEADS
NSA_DH = 128
NSA_CMP_LEN = 32
NSA_CMP_STRIDE = 16
NSA_SEL_LEN = 64
NSA_SEL_TOPK = 16
NSA_SEL_LOCAL = 2
NSA_WINDOW = 512
NSA_FORCE = 1e9

SUM_ROWS = 16

CONV_WIDTH = 31
CONV_HALO = 32

GLA_HEADS = 4
GLA_DK = 256
GLA_DV = 512
GLA_GATE_RANK = 16
GLA_GATE_NORM = 16.0
GLA_CHUNK = 64

_LOG2E = 1.4426950408889634
_SEL_SHIFT = NSA_SEL_LEN.bit_length() - 1
_CHUNK_SHIFT = GLA_CHUNK.bit_length() - 1

_NT = (((1,), (1,)), ((), ()))
_TN = (((0,), (0,)), ((), ()))


def _params(*sem):
    return pltpu.CompilerParams(dimension_semantics=sem, vmem_limit_bytes=VMEM_LIMIT)


def _sigmoid(x):
    return 1.0 / (1.0 + jnp.exp(-x))


def _rms_bf16(x, g):
    ms = jnp.mean(x * x, axis=-1, keepdims=True)
    return (x * lax.rsqrt(ms + NORM_EPS) * g).astype(BF)


def _proj_kernel(x_ref, g_ref, w_ref, wa_ref, o_ref, oa_ref, xn_ref):
    @pl.when(pl.program_id(1) == 0)
    def _():
        xn = _rms_bf16(x_ref[...], g_ref[...])
        xn_ref[...] = xn
        oa_ref[...] = jnp.dot(xn, wa_ref[...], preferred_element_type=F32)

    o_ref[...] = jnp.dot(xn_ref[...], w_ref[...], preferred_element_type=F32).astype(o_ref.dtype)


def _proj(x, gain, w, w_aux, *, tm, tn):
    M, K = x.shape
    N = w.shape[1]
    Na = w_aux.shape[1]
    return pl.pallas_call(
        _proj_kernel,
        grid=(M // tm, N // tn),
        in_specs=[
            pl.BlockSpec((tm, K), lambda i, j: (i, 0)),
            pl.BlockSpec((1, K), lambda i, j: (0, 0)),
            pl.BlockSpec((K, tn), lambda i, j: (0, j)),
            pl.BlockSpec((K, Na), lambda i, j: (0, 0), pipeline_mode=pl.Buffered(1)),
        ],
        out_specs=[
            pl.BlockSpec((tm, tn), lambda i, j: (i, j)),
            pl.BlockSpec((tm, Na), lambda i, j: (i, 0)),
        ],
        out_shape=[jax.ShapeDtypeStruct((M, N), BF), jax.ShapeDtypeStruct((M, Na), F32)],
        scratch_shapes=[pltpu.VMEM((tm, K), BF)],
        compiler_params=_params("parallel", "arbitrary"),
        name="proj",
    )(x, gain.reshape(1, K), w, w_aux)


def _glu_kernel(x_ref, g_ref, wa_ref, wg_ref, o_ref, xn_ref):
    @pl.when(pl.program_id(1) == 0)
    def _():
        xn_ref[...] = _rms_bf16(x_ref[...], g_ref[...])

    xn = xn_ref[...]
    a = jnp.dot(xn, wa_ref[...], preferred_element_type=F32)
    g = jnp.dot(xn, wg_ref[...], preferred_element_type=F32)
    o_ref[...] = a * _sigmoid(g)


def _glu_proj(x, gain, w, *, tm, tn):
    M, K = x.shape
    C = w.shape[1] // 2
    nj = C // tn
    return pl.pallas_call(
        _glu_kernel,
        grid=(M // tm, nj),
        in_specs=[
            pl.BlockSpec((tm, K), lambda i, j: (i, 0)),
            pl.BlockSpec((1, K), lambda i, j: (0, 0)),
            pl.BlockSpec((K, tn), lambda i, j: (0, j)),
            pl.BlockSpec((K, tn), lambda i, j: (0, j + nj)),
        ],
        out_specs=pl.BlockSpec((tm, tn), lambda i, j: (i, j)),
        out_shape=jax.ShapeDtypeStruct((M, C), F32),
        scratch_shapes=[pltpu.VMEM((tm, K), BF)],
        compiler_params=_params("parallel", "arbitrary"),
        name="glu_proj",
    )(x, gain.reshape(1, K), w, w)


def _outproj_kernel(a_ref, w_ref, h_ref, o_ref):
    o_ref[...] = h_ref[...] + jnp.dot(a_ref[...], w_ref[...], preferred_element_type=F32)


def _outproj(a, w, h, *, tm):
    M, K = a.shape
    N = w.shape[1]
    return pl.pallas_call(
        _outproj_kernel,
        grid=(M // tm,),
        in_specs=[
            pl.BlockSpec((tm, K), lambda i: (i, 0)),
            pl.BlockSpec((K, N), lambda i: (0, 0)),
            pl.BlockSpec((tm, N), lambda i: (i, 0)),
        ],
        out_specs=pl.BlockSpec((tm, N), lambda i: (i, 0)),
        out_shape=jax.ShapeDtypeStruct((M, N), F32),
        compiler_params=_params("parallel"),
        name="outproj",
    )(a, w, h)


def _ffn_kernel(x_ref, g_ref, w1_ref, w2_ref, o_ref, xn_ref):
    @pl.when(pl.program_id(1) == 0)
    def _():
        x = x_ref[...]
        xn_ref[...] = _rms_bf16(x, g_ref[...])
        o_ref[...] = x

    a = jnp.dot(xn_ref[...], w1_ref[...], preferred_element_type=F32)
    a = jnp.square(jnp.maximum(a, 0.0)).astype(BF)
    o_ref[...] += jnp.dot(a, w2_ref[...], preferred_element_type=F32)


def _ffn(x, gain, w1, w2, layer, *, tm, tf):
    M, K = x.shape
    Fd = w1.shape[2]
    return pl.pallas_call(
        _ffn_kernel,
        grid=(M // tm, Fd // tf),
        in_specs=[
            pl.BlockSpec((tm, K), lambda i, j: (i, 0)),
            pl.BlockSpec((1, K), lambda i, j: (0, 0)),
            pl.BlockSpec((None, K, tf), lambda i, j: (layer, 0, j)),
            pl.BlockSpec((None, tf, K), lambda i, j: (layer, j, 0)),
        ],
        out_specs=pl.BlockSpec((tm, K), lambda i, j: (i, 0)),
        out_shape=jax.ShapeDtypeStruct((M, K), F32),
        scratch_shapes=[pltpu.VMEM((tm, K), BF)],
        compiler_params=_params("parallel", "arbitrary"),
        name="ffn",
    )(x, gain.reshape(1, K), w1, w2)


def _ple_kernel(h_ref, p_ref, g_ref, wg_ref, wp_ref, gf_ref, o_ref, *, final):
    h = h_ref[...]
    gate = _sigmoid(jnp.dot(_rms_bf16(h, g_ref[...]), wg_ref[...], preferred_element_type=F32))
    proj = jnp.dot(p_ref[...].astype(BF), wp_ref[...], preferred_element_type=F32)
    out = h + proj * gate
    if final:
        ms = jnp.mean(out * out, axis=-1, keepdims=True)
        out = out * lax.rsqrt(ms + NORM_EPS) * gf_ref[...]
    o_ref[...] = out


def _ple(h, p, gain, w_gate, w_proj, layer, gain_final, *, final, tm):
    M, K = h.shape
    Pd = p.shape[2]
    return pl.pallas_call(
        functools.partial(_ple_kernel, final=final),
        grid=(M // tm,),
        in_specs=[
            pl.BlockSpec((tm, K), lambda i: (i, 0)),
            pl.BlockSpec((None, tm, Pd), lambda i: (layer, i, 0)),
            pl.BlockSpec((1, K), lambda i: (0, 0)),
            pl.BlockSpec((None, K, K), lambda i: (layer, 0, 0)),
            pl.BlockSpec((None, Pd, K), lambda i: (layer, 0, 0)),
            pl.BlockSpec((1, K), lambda i: (0, 0)),
        ],
        out_specs=pl.BlockSpec((tm, K), lambda i: (i, 0)),
        out_shape=jax.ShapeDtypeStruct((M, K), F32),
        compiler_params=_params("parallel"),
        name="ple",
    )(h, p, gain.reshape(1, K), w_gate, w_proj, gain_final.reshape(1, K))


def _compress_kernel(x_ref, pos_ref, w1_ref, w2_ref, o_ref):
    Dh = x_ref.shape[-1]
    R = x_ref.shape[0] // NSA_CMP_STRIDE
    nl = NSA_CMP_STRIDE
    a = jnp.zeros((R, w1_ref.shape[-1]), F32)
    b = jnp.zeros((R, w1_ref.shape[-1]), F32)
    for l in range(nl):
        xl = x_ref[pl.ds(l, R, stride=nl), :]
        a = a + jnp.dot((xl + pos_ref[0, l:l + 1, :]).astype(BF),
                        w1_ref[0, l * Dh:(l + 1) * Dh, :], preferred_element_type=F32)
        b = b + jnp.dot((xl + pos_ref[0, nl + l:nl + l + 1, :]).astype(BF),
                        w1_ref[0, (nl + l) * Dh:(nl + l + 1) * Dh, :], preferred_element_type=F32)
    hid = a + pltpu.roll(b, R - 1, 0)
    hid = hid * _sigmoid(hid)
    out = jnp.dot(hid.astype(BF), w2_ref[0], preferred_element_type=F32)
    row = lax.broadcasted_iota(jnp.int32, out.shape, 0)
    o_ref[0, 0, 0] = jnp.where(row < R - 1, out, 0.0).astype(BF)


def _compress(aux, pos, w1, w2, *, B, S, col0):
    G, Dh = NSA_KV_HEADS, NSA_DH
    R = S // NSA_CMP_STRIDE
    return pl.pallas_call(
        _compress_kernel,
        grid=(B, 2, G),
        in_specs=[
            pl.BlockSpec((S, Dh), lambda b, c, g: (b, col0 + c * G + g)),
            pl.BlockSpec((1,) + pos.shape[1:], lambda b, c, g: (c, 0, 0)),
            pl.BlockSpec((1,) + w1.shape[1:], lambda b, c, g: (c, 0, 0)),
            pl.BlockSpec((1,) + w2.shape[1:], lambda b, c, g: (c, 0, 0)),
        ],
        out_specs=pl.BlockSpec((1, 1, 1, R, Dh), lambda b, c, g: (b, c, g, 0, 0)),
        out_shape=jax.ShapeDtypeStruct((B, 2, G, R, Dh), BF),
        compiler_params=_params("parallel", "parallel", "parallel"),
        name="nsa_compress",
    )(aux, pos, w1, w2)


def _nsa_kernel(q_ref, gt_ref, kc_ref, vc_ref, ks_ref, vs_ref, kw_ref, vw_ref, ovt_ref, o_ref,
                ksa_ref, vst_ref, vwt_ref, m_ref, acc_ref, *, tq, tk, n_top):
    Dh = NSA_DH
    i = pl.program_id(2)
    t0 = i * tq
    S = ks_ref.shape[0]
    tc = 512

    @pl.when(i == 0)
    def _():
        lane = lax.broadcasted_iota(jnp.int32, (tc, LANE), 1)
        krow = lax.broadcasted_iota(jnp.int32, (tc, LANE), 0)
        ones_rows = jnp.where(lax.broadcasted_iota(jnp.int32, (SUM_ROWS, tc), 0) == 0, 1.0, 0.0).astype(BF)
        for c in range(S // tc):
            sl = slice(c * tc, (c + 1) * tc)
            vst_ref[0:Dh, sl] = vs_ref[sl, :].astype(F32).T.astype(BF)
            vst_ref[Dh:Dh + SUM_ROWS, sl] = ones_rows
            vwt_ref[0:Dh, sl] = vw_ref[sl, :].astype(F32).T.astype(BF)
            vwt_ref[Dh:Dh + SUM_ROWS, sl] = ones_rows
            ksa_ref[sl, 0:Dh] = ks_ref[sl, :]
            ksa_ref[sl, Dh:Dh + LANE] = jnp.where(
                jnp.right_shift(c * tc + krow, _SEL_SHIFT) == lane, 1.0, 0.0).astype(BF)

    H = NSA_HPG
    tcol = t0 + lax.broadcasted_iota(jnp.int32, (1, tq), 1)
    qst = jnp.concatenate([q_ref[:, h * Dh:(h + 1) * Dh] for h in range(H)], axis=0)
    gtT = _sigmoid(gt_ref[...]).T

    def heads(x):
        return jnp.concatenate([x] * H, axis=1)

    kc = kc_ref[0, 0, 0]
    vcT = vc_ref[0, 0, 0].astype(F32).T.astype(BF)
    ncp = kc.shape[0]
    n_row = lax.broadcasted_iota(jnp.int32, (ncp, tq), 0)
    valid_c = heads((n_row * NSA_CMP_STRIDE + (NSA_CMP_LEN - 1)) <= tcol)
    s = lax.dot_general(kc, qst, _NT, preferred_element_type=F32)
    s = jnp.where(valid_c, s, NEG)
    e = jnp.exp2(s - jnp.max(s, axis=0, keepdims=True))
    den = jnp.sum(e, axis=0, keepdims=True)
    p = jnp.where(valid_c, e, 0.0) * (1.0 / den)
    o_c = jnp.dot(vcT, p.astype(BF), preferred_element_type=F32)
    psum = p[:, 0:tq]
    for h in range(1, H):
        psum = psum + p[:, h * tq:(h + 1) * tq]

    ovt = ovt_ref[...]
    p_hi = psum.astype(BF)
    p_lo = (psum - p_hi.astype(F32)).astype(BF)
    imp = (jnp.dot(ovt, p_hi, preferred_element_type=F32)
           + jnp.dot(ovt, p_lo, preferred_element_type=F32))
    n_sel = imp.shape[0]
    m_idx = lax.broadcasted_iota(jnp.int32, imp.shape, 0)
    m_idx_f = m_idx.astype(F32)
    diff = jnp.right_shift(tcol, _SEL_SHIFT) - m_idx
    forced = (m_idx == 0) | ((diff >= 0) & (diff < NSA_SEL_LOCAL))
    member = jnp.where(forced, 1.0, 0.0)
    work = jnp.where(forced, -3e38, jnp.where(diff >= 0, imp, -NSA_FORCE))
    for _ in range(n_top - 1 - NSA_SEL_LOCAL):
        mx = jnp.max(work, axis=0, keepdims=True)
        first = jnp.min(jnp.where(work == mx, m_idx_f, 1e6), axis=0, keepdims=True)
        pick = m_idx_f == first
        member = jnp.where(pick, 1.0, member)
        work = jnp.where(pick, -3e38, work)
    sel_bias = jnp.where((member > 0.5) & (diff >= 0), 0.0, NEG)
    sel_bias = jnp.concatenate([sel_bias, jnp.zeros((LANE - n_sel, tq), F32)], axis=0)
    sb = sel_bias.T.astype(BF)
    q_aug = jnp.concatenate([qst, jnp.concatenate([sb] * H, axis=0)], axis=1)

    m_ref[...] = jnp.full(m_ref.shape, NEG, F32)
    acc_ref[...] = jnp.zeros(acc_ref.shape, F32)
    key_row = lax.broadcasted_iota(jnp.int32, (tk, tq), 0)

    hw = (H // 2) * tq

    def sel_tile(j, diagonal):
        k0 = pl.multiple_of(j * tk, tk)
        ka = ksa_ref[pl.ds(k0, tk), :]
        va = vst_ref[:, pl.ds(k0, tk)]
        ss = [lax.dot_general(ka, q_aug[c * hw:(c + 1) * hw], _NT, preferred_element_type=F32)
              for c in range(2)]
        if diagonal:
            causal = jnp.concatenate([(k0 + key_row) <= tcol] * (H // 2), axis=1)
        for c in range(2):
            s = ss[c]
            sl = slice(c * hw, (c + 1) * hw)
            if diagonal:
                s = jnp.where(causal, s, NEG)
            m_prev = m_ref[:, sl]
            m_new = jnp.maximum(m_prev, jnp.max(s, axis=0, keepdims=True))
            alpha = jnp.exp2(m_prev - m_new)
            e = jnp.exp2(s - m_new).astype(BF)
            acc_ref[:, sl] = alpha * acc_ref[:, sl] + jnp.dot(va, e, preferred_element_type=F32)
            m_ref[:, sl] = m_new

    def sel_pair(jj, carry):
        sel_tile(2 * jj, False)
        sel_tile(2 * jj + 1, False)
        return carry

    n_full = (t0 + tq + tk - 1) // tk - 1
    lax.fori_loop(0, n_full // 2, sel_pair, 0)

    @pl.when(n_full % 2 == 1)
    def _():
        sel_tile(n_full - 1, False)

    sel_tile(n_full, True)

    wlen = NSA_WINDOW + tq
    ws = pl.multiple_of(jnp.maximum(t0 - NSA_WINDOW, 0), tq)
    kwt = kw_ref[pl.ds(ws, wlen), :]
    vwT = vwt_ref[:, pl.ds(ws, wlen)]
    kpos_w = ws + lax.broadcasted_iota(jnp.int32, (wlen, tq), 0)
    ok_w = (kpos_w <= tcol) & (kpos_w > tcol - NSA_WINDOW)
    bias_w = jnp.where(ok_w, 0.0, NEG)
    bias_w2 = jnp.concatenate([bias_w] * (H // 2), axis=1)
    sw = [lax.dot_general(kwt, qst[c * hw:(c + 1) * hw], _NT, preferred_element_type=F32) + bias_w2
          for c in range(2)]
    o_w = []
    for c in range(2):
        e = jnp.exp2(sw[c] - jnp.max(sw[c], axis=0, keepdims=True)).astype(BF)
        ow = jnp.dot(vwT, e, preferred_element_type=F32)
        o_w.append(ow[0:Dh] * (1.0 / ow[Dh:Dh + 1]))
    o_w = jnp.concatenate(o_w, axis=1)
    o_s = acc_ref[0:Dh, :] * (1.0 / acc_ref[Dh:Dh + 1, :])

    for h in range(H):
        sl = slice(h * tq, (h + 1) * tq)
        out = (gtT[3 * h:3 * h + 1, :] * o_c[:, sl] + gtT[3 * h + 1:3 * h + 2, :] * o_s[:, sl]
               + gtT[3 * h + 2:3 * h + 3, :] * o_w[:, sl])
        o_ref[:, h * Dh:(h + 1) * Dh] = out.T.astype(BF)


def _nsa_core(P, gates, cmp_kv, ovt, *, B, S, tq, tk):
    G, HPG, Dh = NSA_KV_HEADS, NSA_HPG, NSA_DH
    nq = S // tq
    ncp = cmp_kv.shape[3]
    n_sel = S // NSA_SEL_LEN
    n_top = min(NSA_SEL_TOPK, n_sel)
    assert S >= NSA_WINDOW + tq and NSA_WINDOW % tq == 0 and S % tk == 0 and tk % NSA_SEL_LEN == 0
    assert S % 512 == 0
    qcols = NSA_HEADS * Dh // LANE
    kvcols = G

    def kv_spec(slab):
        off = qcols + slab * kvcols
        return pl.BlockSpec((S, Dh), lambda b, g, i: (b, off + g))

    return pl.pallas_call(
        functools.partial(_nsa_kernel, tq=tq, tk=tk, n_top=n_top),
        grid=(B, G, nq),
        in_specs=[
            pl.BlockSpec((tq, HPG * Dh), lambda b, g, i: (b * nq + i, g)),
            pl.BlockSpec((tq, LANE), lambda b, g, i: (b * nq + i, g)),
            pl.BlockSpec((1, 1, 1, ncp, Dh), lambda b, g, i: (b, 0, g, 0, 0)),
            pl.BlockSpec((1, 1, 1, ncp, Dh), lambda b, g, i: (b, 1, g, 0, 0)),
            kv_spec(0), kv_spec(1), kv_spec(2), kv_spec(3),
            pl.BlockSpec(ovt.shape, lambda b, g, i: (0, 0)),
        ],
        out_specs=pl.BlockSpec((tq, HPG * Dh), lambda b, g, i: (b * nq + i, g)),
        out_shape=jax.ShapeDtypeStruct((B * S, NSA_HEADS * Dh), BF),
        scratch_shapes=[
            pltpu.VMEM((S, Dh + LANE), BF),
            pltpu.VMEM((Dh + SUM_ROWS, S), BF),
            pltpu.VMEM((Dh + SUM_ROWS, S), BF),
            pltpu.VMEM((1, HPG * tq), F32),
            pltpu.VMEM((Dh + SUM_ROWS, HPG * tq), F32),
        ],
        compiler_params=_params("parallel", "parallel", "arbitrary"),
        name="nsa_core",
    )(P, gates, cmp_kv, cmp_kv, P, P, P, P, ovt)


def _overlap_matrix_t(S, ncp):
    n_cmp = (S - NSA_CMP_LEN) // NSA_CMP_STRIDE + 1
    n_sel = S // NSA_SEL_LEN
    cs = np.arange(n_cmp) * NSA_CMP_STRIDE
    ss = np.arange(n_sel) * NSA_SEL_LEN
    ovl = ((cs[:, None] <= ss[None, :] + NSA_SEL_LEN - 1)
           & (cs[:, None] + NSA_CMP_LEN - 1 >= ss[None, :])).astype(np.float32)
    out = np.zeros((n_sel, ncp), np.float32)
    out[:, :n_cmp] = ovl.T
    return jnp.asarray(out, BF)


def _nsa_layer(h, gain, w_in, w_out, pos_k, w1_k, w2_k, pos_v, w1_v, w2_v, *, B, S):
    G, HPG, Dh = NSA_KV_HEADS, NSA_HPG, NSA_DH
    D = h.shape[1]
    qw, kvw = NSA_HEADS * Dh, G * Dh
    main = qw + 6 * kvw
    w_main = jnp.concatenate([w_in[:, :qw] * (Dh ** -0.5 * _LOG2E), w_in[:, qw + 2 * kvw:main]],
                             axis=1).astype(BF)
    wg = w_in[:, main:].reshape(D, G, 3 * HPG)
    wg = jnp.pad(wg, ((0, 0), (0, 0), (0, LANE - 3 * HPG))).reshape(D, G * LANE)
    w_aux = jnp.concatenate([wg, w_in[:, qw:qw + 2 * kvw]], axis=1).astype(BF)
    P, aux = _proj(h, gain, w_main, w_aux, tm=512, tn=2048)

    pos = jnp.stack([pos_k, pos_v])
    w1 = jnp.stack([w1_k, w1_v]).astype(BF)
    w2 = jnp.stack([w2_k, w2_v]).astype(BF)
    cmp_kv = _compress(aux, pos, w1, w2, B=B, S=S, col0=G)

    ovt = _overlap_matrix_t(S, S // NSA_CMP_STRIDE)
    o = _nsa_core(P, aux, cmp_kv, ovt, B=B, S=S, tq=256, tk=512)
    return _outproj(o, w_out.astype(BF), h, tm=512)


def _conv_kernel(u_ref, halo_ref, dw_ref, db_ref, lg_ref, lb_ref, w_ref, h_ref, o_ref,
                 buf_ref, sh_ref, y_ref, *, ts):
    C = u_ref.shape[-1]
    halo = halo_ref[...]
    buf_ref[0:CONV_HALO, :] = jnp.where(pl.program_id(1) > 0, halo, 0.0)
    buf_ref[CONV_HALO:CONV_HALO + ts, :] = u_ref[...]
    off = CONV_HALO - (CONV_WIDTH - 1)
    rs = 128
    sub = 8
    span = ts + CONV_HALO - sub

    def chan_body(c, carry):
        c0 = pl.multiple_of(c * LANE, LANE)
        for r in range(1, sub):
            sh_ref[r, 0:span, :] = buf_ref[pl.ds(r, span), pl.ds(c0, LANE)]
        for rr in range(ts // rs):
            acc = jnp.zeros((rs, LANE), F32) + db_ref[:, pl.ds(c0, LANE)]
            for k in range(CONV_WIDTH):
                r = (off + k) % sub
                base = rr * rs + off + k - r
                if r == 0:
                    win = buf_ref[pl.ds(base, rs), pl.ds(c0, LANE)]
                else:
                    win = sh_ref[r, pl.ds(base, rs), :]
                acc = acc + win * dw_ref[k:k + 1, pl.ds(c0, LANE)]
            y_ref[pl.ds(rr * rs, rs), pl.ds(c0, LANE)] = acc
        return carry

    lax.fori_loop(0, C // LANE, chan_body, 0)

    y = y_ref[...]
    mu = jnp.mean(y, axis=-1, keepdims=True)
    yc = y - mu
    var = jnp.mean(yc * yc, axis=-1, keepdims=True)
    z = yc * lax.rsqrt(var + LN_EPS) * lg_ref[...] + lb_ref[...]
    z = (z * _sigmoid(z)).astype(BF)
    o_ref[...] = h_ref[...] + jnp.dot(z, w_ref[...], preferred_element_type=F32)


def _conv_tail(u, dw, db, ln_g, ln_b, w_out, h, *, B, S, ts):
    M, C = u.shape
    D = w_out.shape[1]
    ns = S // ts
    hb = ts // CONV_HALO
    dwp = jnp.pad(dw, ((0, CONV_HALO - CONV_WIDTH), (0, 0)))
    return pl.pallas_call(
        functools.partial(_conv_kernel, ts=ts),
        grid=(B, ns),
        in_specs=[
            pl.BlockSpec((ts, C), lambda b, i: (b * ns + i, 0)),
            pl.BlockSpec((CONV_HALO, C), lambda b, i: (jnp.maximum((b * ns + i) * hb - 1, 0), 0)),
            pl.BlockSpec((CONV_HALO, C), lambda b, i: (0, 0)),
            pl.BlockSpec((1, C), lambda b, i: (0, 0)),
            pl.BlockSpec((1, C), lambda b, i: (0, 0)),
            pl.BlockSpec((1, C), lambda b, i: (0, 0)),
            pl.BlockSpec((C, D), lambda b, i: (0, 0)),
            pl.BlockSpec((ts, D), lambda b, i: (b * ns + i, 0)),
        ],
        out_specs=pl.BlockSpec((ts, D), lambda b, i: (b * ns + i, 0)),
        out_shape=jax.ShapeDtypeStruct((M, D), F32),
        scratch_shapes=[pltpu.VMEM((CONV_HALO + ts, C), F32), pltpu.VMEM((8, CONV_HALO + ts, LANE), F32),
                        pltpu.VMEM((ts, C), F32)],
        compiler_params=_params("parallel", "parallel"),
        name="conv_tail",
    )(u, u, dwp, db.reshape(1, C), ln_g.reshape(1, C), ln_b.reshape(1, C), w_out, h)


def _conv_layer(h, gain, w_in, dw, db, ln_g, ln_b, w_out, *, B, S):
    u = _glu_proj(h, gain, w_in.astype(BF), tm=512, tn=512)
    return _conv_tail(u, dw, db, ln_g, ln_b, w_out.astype(BF), h, B=B, S=S, ts=256)


def _gla_kernel(q_ref, k_ref, v_ref, r_ref, gz_ref, wgu_ref, bg_ref, ng_ref, o_ref, st_ref, *, ts, nh):
    C = GLA_CHUNK
    dk, dv = GLA_DK, GLA_DV

    @pl.when(pl.program_id(2) == 0)
    def _():
        st_ref[...] = jnp.zeros(st_ref.shape, F32)

    z = jnp.dot(gz_ref[...].astype(BF), wgu_ref[...], preferred_element_type=F32) + bg_ref[...]
    glog = (jnp.minimum(z, 0.0) - jnp.log(1.0 + jnp.exp(-jnp.abs(z)))) / GLA_GATE_NORM

    g1 = glog.astype(BF)
    r1 = glog - g1.astype(F32)
    g2 = r1.astype(BF)
    g3 = (r1 - g2.astype(F32)).astype(BF)
    row = lax.broadcasted_iota(jnp.int32, (ts, ts), 0)
    col = lax.broadcasted_iota(jnp.int32, (ts, ts), 1)
    same = jnp.right_shift(row, _CHUNK_SHIFT) == jnp.right_shift(col, _CHUNK_SHIFT)
    tri = jnp.where(same & (col <= row), 1.0, 0.0).astype(BF)
    full = jnp.where(same, 1.0, 0.0).astype(BF)

    def msum(mat):
        return (jnp.dot(mat, g1, preferred_element_type=F32)
                + jnp.dot(mat, g2, preferred_element_type=F32)
                + jnp.dot(mat, g3, preferred_element_type=F32))

    bcum = msum(tri)
    blast = msum(full)

    q = q_ref[...].astype(F32) * (dk ** -0.5)
    k = k_ref[...].astype(F32)
    q_dec = (q * jnp.exp(bcum)).astype(BF)
    k_intra = (k * jnp.exp(-bcum)).astype(BF)
    k_state = (k * jnp.exp(blast - bcum)).astype(BF)
    decay = jnp.exp(blast)
    causal = (lax.broadcasted_iota(jnp.int32, (C, C), 1) <= lax.broadcasted_iota(jnp.int32, (C, C), 0))

    outs = [[] for _ in range(nh)]
    for c in range(ts // C):
        sl = slice(c * C, (c + 1) * C)
        for hh in range(nh):
            ka = slice(hh * dk, (hh + 1) * dk)
            vc = v_ref[sl, hh * dv:(hh + 1) * dv]
            a = lax.dot_general(q_dec[sl, ka], k_intra[sl, ka], _NT, preferred_element_type=F32)
            a = jnp.where(causal, a, 0.0).astype(BF)
            st = st_ref[hh]
            o = (jnp.dot(a, vc, preferred_element_type=F32)
                 + lax.dot_general(q_dec[sl, ka], st.astype(BF), _NT, preferred_element_type=F32))
            outs[hh].append(o)
            st_ref[hh] = st * decay[c * C:c * C + 1, ka] + lax.dot_general(
                vc, k_state[sl, ka], _TN, preferred_element_type=F32)

    for hh in range(nh):
        o = jnp.concatenate(outs[hh], axis=0)
        o = o * lax.rsqrt(jnp.mean(o * o, axis=-1, keepdims=True) + NORM_EPS) * ng_ref[...]
        r = r_ref[:, hh * dv:(hh + 1) * dv].astype(F32)
        o_ref[:, hh * dv:(hh + 1) * dv] = (o * (r * _sigmoid(r))).astype(BF)


def _gla_core(P, gz, wgu, bg, ng, *, B, S, ts, nh):
    H, dk, dv = GLA_HEADS, GLA_DK, GLA_DV
    ns = S // ts
    hg = H // nh
    kq = hg
    vv = 2 * H * dk // (nh * dv)
    rr = vv + hg
    return pl.pallas_call(
        functools.partial(_gla_kernel, ts=ts, nh=nh),
        grid=(B, hg, ns),
        in_specs=[
            pl.BlockSpec((ts, nh * dk), lambda b, hh, i: (b * ns + i, hh)),
            pl.BlockSpec((ts, nh * dk), lambda b, hh, i: (b * ns + i, kq + hh)),
            pl.BlockSpec((ts, nh * dv), lambda b, hh, i: (b * ns + i, vv + hh)),
            pl.BlockSpec((ts, nh * dv), lambda b, hh, i: (b * ns + i, rr + hh)),
            pl.BlockSpec((ts, LANE), lambda b, hh, i: (b * ns + i, 0)),
            pl.BlockSpec((LANE, nh * dk), lambda b, hh, i: (0, hh)),
            pl.BlockSpec((1, nh * dk), lambda b, hh, i: (0, hh)),
            pl.BlockSpec((1, dv), lambda b, hh, i: (0, 0)),
        ],
        out_specs=pl.BlockSpec((ts, nh * dv), lambda b, hh, i: (b * ns + i, hh)),
        out_shape=jax.ShapeDtypeStruct((B * S, H * dv), BF),
        scratch_shapes=[pltpu.VMEM((nh, dv, dk), F32)],
        compiler_params=_params("parallel", "parallel", "arbitrary"),
        name="gla_core",
    )(P, P, P, P, gz, wgu, bg.reshape(1, H * dk), ng.reshape(1, dv))


def _gla_layer(h, gain, w_in, w_gate_up, b_gate, norm_g, w_out, *, B, S):
    H, dk, dv = GLA_HEADS, GLA_DK, GLA_DV
    main = 2 * H * dk + 2 * H * dv
    w_main = w_in[:, :main].astype(BF)
    w_gz = jnp.pad(w_in[:, main:], ((0, 0), (0, LANE - GLA_GATE_RANK))).astype(BF)
    P, gz = _proj(h, gain, w_main, w_gz, tm=512, tn=2048)
    wgu = jnp.pad(w_gate_up, ((0, LANE - GLA_GATE_RANK), (0, 0))).astype(BF)
    o = _gla_core(P, gz, wgu, b_gate, norm_g, B=B, S=S, ts=256, nh=GLA_HEADS)
    return _outproj(o, w_out.astype(BF), h, tm=512)


def kernel(x, p, norm_mix, norm_ffn, norm_ple, norm_final, ffn_w1, ffn_w2, ple_w_proj, ple_w_gate,
           nsa_w_in, nsa_w_out, nsa_cmp_pos_k, nsa_cmp_w1_k, nsa_cmp_w2_k, nsa_cmp_pos_v,
           nsa_cmp_w1_v, nsa_cmp_w2_v, conv_w_in, conv_dw, conv_db, conv_ln_g, conv_ln_b,
           conv_w_out, gla_w_in, gla_w_gate_up, gla_b_gate, gla_norm_g, gla_w_out):
    B, S, D = x.shape
    depth = p.shape[0]
    M = B * S
    h = x.reshape(M, D)
    p2 = p.reshape(depth, M, p.shape[-1])
    ffn_w1, ffn_w2 = ffn_w1.astype(BF), ffn_w2.astype(BF)
    ple_w_gate, ple_w_proj = ple_w_gate.astype(BF), ple_w_proj.astype(BF)
    for i in range(depth):
        m, j = i % 3, i // 3
        if m == 0:
            h = _nsa_layer(h, norm_mix[i], nsa_w_in[j], nsa_w_out[j], nsa_cmp_pos_k[j],
                           nsa_cmp_w1_k[j], nsa_cmp_w2_k[j], nsa_cmp_pos_v[j], nsa_cmp_w1_v[j],
                           nsa_cmp_w2_v[j], B=B, S=S)
        elif m == 1:
            h = _conv_layer(h, norm_mix[i], conv_w_in[j], conv_dw[j], conv_db[j], conv_ln_g[j],
                            conv_ln_b[j], conv_w_out[j], B=B, S=S)
        else:
            h = _gla_layer(h, norm_mix[i], gla_w_in[j], gla_w_gate_up[j], gla_b_gate[j],
                           gla_norm_g[j], gla_w_out[j], B=B, S=S)
        h = _ffn(h, norm_ffn[i], ffn_w1, ffn_w2, i, tm=512, tf=1024)
        h = _ple(h, p2, norm_ple[i], ple_w_gate, ple_w_proj, i, norm_final,
                 final=(i == depth - 1), tm=256)
    return h.reshape(B, S, D)
```

```python
import functools

import numpy as np
import jax
import jax.numpy as jnp
from jax import lax
from jax.experimental import pallas as pl
from jax.experimental.pallas import tpu as pltpu

BF = jnp.bfloat16
F32 = jnp.float32

NORM_EPS = 1e-6
LN_EPS = 1e-5
NEG = -1e30
LANE = 128
VMEM_LIMIT = 56 * 1024 * 1024

NSA_HEADS = 16
NSA_KV_HEADS = 4
NSA_HPG = NSA_HEADS // NSA_KV_HEADS
NSA_DH = 128
NSA_CMP_LEN = 32
NSA_CMP_STRIDE = 16
NSA_SEL_LEN = 64
NSA_SEL_TOPK = 16
NSA_SEL_LOCAL = 2
NSA_WINDOW = 512
NSA_FORCE = 1e9

SUM_ROWS = 16

CONV_WIDTH = 31
CONV_HALO = 32

GLA_HEADS = 4
GLA_DK = 256
GLA_DV = 512
GLA_GATE_RANK = 16
GLA_GATE_NORM = 16.0
GLA_CHUNK = 64

TM = 512
TM_FFN = 1024
TM_PLE = 256
TN_PROJ = 2048
TN_GLU = 1024
TF_FFN = 512
TQ_NSA = 256
TK_NSA = 512
TS_SEQ = 256

_LOG2E = 1.4426950408889634
_SEL_SHIFT = NSA_SEL_LEN.bit_length() - 1
_CHUNK_SHIFT = GLA_CHUNK.bit_length() - 1

_NT = (((1,), (1,)), ((), ()))
_TN = (((0,), (0,)), ((), ()))


def _params(*sem):
    return pltpu.CompilerParams(dimension_semantics=sem, vmem_limit_bytes=VMEM_LIMIT)


def _sigmoid(x):
    return 1.0 / (1.0 + jnp.exp(-x))


def _rms_bf16(x, g):
    ms = jnp.mean(x * x, axis=-1, keepdims=True)
    return (x * lax.rsqrt(ms + NORM_EPS) * g).astype(BF)


def _proj_kernel(x_ref, g_ref, w_ref, wa_ref, o_ref, oa_ref, xn_ref):
    @pl.when(pl.program_id(1) == 0)
    def _():
        xn = _rms_bf16(x_ref[...], g_ref[...])
        xn_ref[...] = xn
        oa_ref[...] = jnp.dot(xn, wa_ref[...], preferred_element_type=F32)

    o_ref[...] = jnp.dot(xn_ref[...], w_ref[...], preferred_element_type=F32).astype(o_ref.dtype)


def _proj(x, gain, w, w_aux, *, tm, tn):
    M, K = x.shape
    N = w.shape[1]
    Na = w_aux.shape[1]
    return pl.pallas_call(
        _proj_kernel,
        grid=(M // tm, N // tn),
        in_specs=[
            pl.BlockSpec((tm, K), lambda i, j: (i, 0)),
            pl.BlockSpec((1, K), lambda i, j: (0, 0)),
            pl.BlockSpec((K, tn), lambda i, j: (0, j)),
            pl.BlockSpec((K, Na), lambda i, j: (0, 0), pipeline_mode=pl.Buffered(1)),
        ],
        out_specs=[
            pl.BlockSpec((tm, tn), lambda i, j: (i, j)),
            pl.BlockSpec((tm, Na), lambda i, j: (i, 0)),
        ],
        out_shape=[jax.ShapeDtypeStruct((M, N), BF), jax.ShapeDtypeStruct((M, Na), F32)],
        scratch_shapes=[pltpu.VMEM((tm, K), BF)],
        compiler_params=_params("parallel", "arbitrary"),
        name="proj",
    )(x, gain.reshape(1, K), w, w_aux)


def _glu_kernel(x_ref, g_ref, wa_ref, wg_ref, o_ref, xn_ref):
    @pl.when(pl.program_id(1) == 0)
    def _():
        xn_ref[...] = _rms_bf16(x_ref[...], g_ref[...])

    xn = xn_ref[...]
    a = jnp.dot(xn, wa_ref[...], preferred_element_type=F32)
    g = jnp.dot(xn, wg_ref[...], preferred_element_type=F32)
    o_ref[...] = a * _sigmoid(g)


def _glu_proj(x, gain, w, *, tm, tn):
    M, K = x.shape
    C = w.shape[1] // 2
    nj = C // tn
    return pl.pallas_call(
        _glu_kernel,
        grid=(M // tm, nj),
        in_specs=[
            pl.BlockSpec((tm, K), lambda i, j: (i, 0)),
            pl.BlockSpec((1, K), lambda i, j: (0, 0)),
            pl.BlockSpec((K, tn), lambda i, j: (0, j)),
            pl.BlockSpec((K, tn), lambda i, j: (0, j + nj)),
        ],
        out_specs=pl.BlockSpec((tm, tn), lambda i, j: (i, j)),
        out_shape=jax.ShapeDtypeStruct((M, C), F32),
        scratch_shapes=[pltpu.VMEM((tm, K), BF)],
        compiler_params=_params("parallel", "arbitrary"),
        name="glu_proj",
    )(x, gain.reshape(1, K), w, w)


def _outproj_kernel(a_ref, w_ref, h_ref, o_ref):
    o_ref[...] = h_ref[...] + jnp.dot(a_ref[...], w_ref[...], preferred_element_type=F32)


def _outproj(a, w, h, *, tm):
    M, K = a.shape
    N = w.shape[1]
    return pl.pallas_call(
        _outproj_kernel,
        grid=(M // tm,),
        in_specs=[
            pl.BlockSpec((tm, K), lambda i: (i, 0)),
            pl.BlockSpec((K, N), lambda i: (0, 0)),
            pl.BlockSpec((tm, N), lambda i: (i, 0)),
        ],
        out_specs=pl.BlockSpec((tm, N), lambda i: (i, 0)),
        out_shape=jax.ShapeDtypeStruct((M, N), F32),
        compiler_params=_params("parallel"),
        name="outproj",
    )(a, w, h)


def _ffn_kernel(x_ref, g_ref, w1_ref, w2_ref, o_ref, xn_ref):
    @pl.when(pl.program_id(1) == 0)
    def _():
        x = x_ref[...]
        xn_ref[...] = _rms_bf16(x, g_ref[...])
        o_ref[...] = x

    a = jnp.dot(xn_ref[...], w1_ref[...], preferred_element_type=F32)
    a = jnp.square(jnp.maximum(a, 0.0)).astype(BF)
    o_ref[...] += jnp.dot(a, w2_ref[...], preferred_element_type=F32)


def _ffn(x, gain, w1, w2, layer, *, tm, tf):
    M, K = x.shape
    Fd = w1.shape[2]
    return pl.pallas_call(
        _ffn_kernel,
        grid=(M // tm, Fd // tf),
        in_specs=[
            pl.BlockSpec((tm, K), lambda i, j: (i, 0)),
            pl.BlockSpec((1, K), lambda i, j: (0, 0)),
            pl.BlockSpec((None, K, tf), lambda i, j: (layer, 0, j)),
            pl.BlockSpec((None, tf, K), lambda i, j: (layer, j, 0)),
        ],
        out_specs=pl.BlockSpec((tm, K), lambda i, j: (i, 0)),
        out_shape=jax.ShapeDtypeStruct((M, K), F32),
        scratch_shapes=[pltpu.VMEM((tm, K), BF)],
        compiler_params=_params("parallel", "arbitrary"),
        name="ffn",
    )(x, gain.reshape(1, K), w1, w2)


def _ple_kernel(h_ref, p_ref, g_ref, wg_ref, wp_ref, gf_ref, o_ref, *, final):
    h = h_ref[...]
    gate = _sigmoid(jnp.dot(_rms_bf16(h, g_ref[...]), wg_ref[...], preferred_element_type=F32))
    proj = jnp.dot(p_ref[...].astype(BF), wp_ref[...], preferred_element_type=F32)
    out = h + proj * gate
    if final:
        ms = jnp.mean(out * out, axis=-1, keepdims=True)
        out = out * lax.rsqrt(ms + NORM_EPS) * gf_ref[...]
    o_ref[...] = out


def _ple(h, p, gain, w_gate, w_proj, layer, gain_final, *, final, tm):
    M, K = h.shape
    Pd = p.shape[2]
    return pl.pallas_call(
        functools.partial(_ple_kernel, final=final),
        grid=(M // tm,),
        in_specs=[
            pl.BlockSpec((tm, K), lambda i: (i, 0)),
            pl.BlockSpec((None, tm, Pd), lambda i: (layer, i, 0)),
            pl.BlockSpec((1, K), lambda i: (0, 0)),
            pl.BlockSpec((None, K, K), lambda i: (layer, 0, 0)),
            pl.BlockSpec((None, Pd, K), lambda i: (layer, 0, 0)),
            pl.BlockSpec((1, K), lambda i: (0, 0)),
        ],
        out_specs=pl.BlockSpec((tm, K), lambda i: (i, 0)),
        out_shape=jax.ShapeDtypeStruct((M, K), F32),
        compiler_params=_params("parallel"),
        name="ple",
    )(h, p, gain.reshape(1, K), w_gate, w_proj, gain_final.reshape(1, K))


def _compress_kernel(x_ref, pos_ref, w1_ref, w2_ref, o_ref):
    Dh = x_ref.shape[-1]
    R = x_ref.shape[0] // NSA_CMP_STRIDE
    nl = NSA_CMP_STRIDE
    a = jnp.zeros((R, w1_ref.shape[-1]), F32)
    b = jnp.zeros((R, w1_ref.shape[-1]), F32)
    for l in range(nl):
        xl = x_ref[pl.ds(l, R, stride=nl), :]
        a = a + jnp.dot((xl + pos_ref[0, l:l + 1, :]).astype(BF),
                        w1_ref[0, l * Dh:(l + 1) * Dh, :], preferred_element_type=F32)
        b = b + jnp.dot((xl + pos_ref[0, nl + l:nl + l + 1, :]).astype(BF),
                        w1_ref[0, (nl + l) * Dh:(nl + l + 1) * Dh, :], preferred_element_type=F32)
    hid = a + pltpu.roll(b, R - 1, 0)
    hid = hid * _sigmoid(hid)
    out = jnp.dot(hid.astype(BF), w2_ref[0], preferred_element_type=F32)
    row = lax.broadcasted_iota(jnp.int32, out.shape, 0)
    o_ref[0, 0, 0] = jnp.where(row < R - 1, out, 0.0).astype(BF)


def _compress(aux, pos, w1, w2, *, B, S, col0):
    G, Dh = NSA_KV_HEADS, NSA_DH
    R = S // NSA_CMP_STRIDE
    return pl.pallas_call(
        _compress_kernel,
        grid=(B, 2, G),
        in_specs=[
            pl.BlockSpec((S, Dh), lambda b, c, g: (b, col0 + c * G + g)),
            pl.BlockSpec((1,) + pos.shape[1:], lambda b, c, g: (c, 0, 0)),
            pl.BlockSpec((1,) + w1.shape[1:], lambda b, c, g: (c, 0, 0)),
            pl.BlockSpec((1,) + w2.shape[1:], lambda b, c, g: (c, 0, 0)),
        ],
        out_specs=pl.BlockSpec((1, 1, 1, R, Dh), lambda b, c, g: (b, c, g, 0, 0)),
        out_shape=jax.ShapeDtypeStruct((B, 2, G, R, Dh), BF),
        compiler_params=_params("parallel", "parallel", "parallel"),
        name="nsa_compress",
    )(aux, pos, w1, w2)


def _nsa_kernel(q_ref, gt_ref, kc_ref, vc_ref, ks_ref, vs_ref, kw_ref, vw_ref, ovt_ref, o_ref,
                ksa_ref, vst_ref, vwt_ref, m_ref, acc_ref, sa_ref, sb_ref, ma_ref, mb_ref, ow_ref, *, tq, tk, n_top):
    Dh = NSA_DH
    i = pl.program_id(2)
    t0 = i * tq
    S = ks_ref.shape[0]
    tc = 512

    @pl.when(i == 0)
    def _():
        lane = lax.broadcasted_iota(jnp.int32, (tc, LANE), 1)
        krow = lax.broadcasted_iota(jnp.int32, (tc, LANE), 0)
        ones_rows = jnp.where(lax.broadcasted_iota(jnp.int32, (SUM_ROWS, tc), 0) == 0, 1.0, 0.0).astype(BF)
        for c in range(S // tc):
            sl = slice(c * tc, (c + 1) * tc)
            vst_ref[0:Dh, sl] = vs_ref[sl, :].astype(F32).T.astype(BF)
            vst_ref[Dh:Dh + SUM_ROWS, sl] = ones_rows
            vwt_ref[0:Dh, sl] = vw_ref[sl, :].astype(F32).T.astype(BF)
            vwt_ref[Dh:Dh + SUM_ROWS, sl] = ones_rows
            ksa_ref[sl, 0:Dh] = ks_ref[sl, :]
            ksa_ref[sl, Dh:Dh + LANE] = jnp.where(
                jnp.right_shift(c * tc + krow, _SEL_SHIFT) == lane, 1.0, 0.0).astype(BF)

    H = NSA_HPG
    tcol = t0 + lax.broadcasted_iota(jnp.int32, (1, tq), 1)
    qst = jnp.concatenate([q_ref[:, h * Dh:(h + 1) * Dh] for h in range(H)], axis=0)
    gtT = _sigmoid(gt_ref[...]).T

    def heads(x):
        return jnp.concatenate([x] * H, axis=1)

    kc = kc_ref[0, 0, 0]
    vcT = vc_ref[0, 0, 0].astype(F32).T.astype(BF)
    ncp = kc.shape[0]
    n_row = lax.broadcasted_iota(jnp.int32, (ncp, tq), 0)
    valid_c = heads((n_row * NSA_CMP_STRIDE + (NSA_CMP_LEN - 1)) <= tcol)
    s = lax.dot_general(kc, qst, _NT, preferred_element_type=F32)
    s = jnp.where(valid_c, s, NEG)
    e = jnp.exp2(s - jnp.max(s, axis=0, keepdims=True))
    den = jnp.sum(e, axis=0, keepdims=True)
    p = jnp.where(valid_c, e, 0.0) * (1.0 / den)
    o_c = jnp.dot(vcT, p.astype(BF), preferred_element_type=F32)
    psum = p[:, 0:tq]
    for h in range(1, H):
        psum = psum + p[:, h * tq:(h + 1) * tq]

    hw = (H // 2) * tq

    wlen = NSA_WINDOW + tq
    ws = pl.multiple_of(jnp.maximum(t0 - NSA_WINDOW, 0), tq)
    kwt = kw_ref[pl.ds(ws, wlen), :]
    vwT = vwt_ref[:, pl.ds(ws, wlen)]
    kpos_w = ws + lax.broadcasted_iota(jnp.int32, (wlen, tq), 0)
    ok_w = (kpos_w <= tcol) & (kpos_w > tcol - NSA_WINDOW)
    bias_w = jnp.where(ok_w, 0.0, NEG)
    bias_w2 = jnp.concatenate([bias_w] * (H // 2), axis=1)
    sw = [lax.dot_general(kwt, qst[c * hw:(c + 1) * hw], _NT, preferred_element_type=F32) + bias_w2
          for c in range(2)]
    o_w = []
    for c in range(2):
        e = jnp.exp2(sw[c] - jnp.max(sw[c], axis=0, keepdims=True)).astype(BF)
        ow = jnp.dot(vwT, e, preferred_element_type=F32)
        o_w.append(ow[0:Dh] * (1.0 / ow[Dh:Dh + 1]))
    ow_ref[...] = jnp.concatenate(o_w, axis=1)

    ovt = ovt_ref[...]
    p_hi = psum.astype(BF)
    p_lo = (psum - p_hi.astype(F32)).astype(BF)
    imp = (jnp.dot(ovt, p_hi, preferred_element_type=F32)
           + jnp.dot(ovt, p_lo, preferred_element_type=F32))
    n_sel = imp.shape[0]
    m_idx = lax.broadcasted_iota(jnp.int32, imp.shape, 0)
    m_idx_f = m_idx.astype(F32)
    diff = jnp.right_shift(tcol, _SEL_SHIFT) - m_idx
    forced = (m_idx == 0) | ((diff >= 0) & (diff < NSA_SEL_LOCAL))
    member = jnp.where(forced, 1.0, 0.0)
    work = jnp.where(forced, -3e38, jnp.where(diff >= 0, imp, -NSA_FORCE))
    for _ in range(n_top - 1 - NSA_SEL_LOCAL):
        mx = jnp.max(work, axis=0, keepdims=True)
        first = jnp.min(jnp.where(work == mx, m_idx_f, 1e6), axis=0, keepdims=True)
        pick = m_idx_f == first
        member = jnp.where(pick, 1.0, member)
        work = jnp.where(pick, -3e38, work)
    sel_bias = jnp.where((member > 0.5) & (diff >= 0), 0.0, NEG)
    sel_bias = jnp.concatenate([sel_bias, jnp.zeros((LANE - n_sel, tq), F32)], axis=0)
    sb = sel_bias.T.astype(BF)
    q_aug = jnp.concatenate([qst, jnp.concatenate([sb] * H, axis=0)], axis=1)

    m_ref[...] = jnp.full(m_ref.shape, NEG, F32)
    acc_ref[...] = jnp.zeros(acc_ref.shape, F32)
    key_row = lax.broadcasted_iota(jnp.int32, (tk, tq), 0)

    def produce(j, s_ref, mt_ref, diagonal):
        k0 = pl.multiple_of(j * tk, tk)
        s = lax.dot_general(ksa_ref[pl.ds(k0, tk), :], q_aug, _NT, preferred_element_type=F32)
        if diagonal:
            s = jnp.where(heads((k0 + key_row) <= tcol), s, NEG)
        s_ref[...] = s
        mt_ref[...] = jnp.max(s, axis=0, keepdims=True)

    def consume(j, s_ref, mt_ref):
        k0 = pl.multiple_of(j * tk, tk)
        va = vst_ref[:, pl.ds(k0, tk)]
        for c in range(2):
            sl = slice(c * hw, (c + 1) * hw)
            m_prev = m_ref[:, sl]
            m_new = jnp.maximum(m_prev, mt_ref[:, sl])
            alpha = jnp.exp2(m_prev - m_new)
            e = jnp.exp2(s_ref[:, sl] - m_new).astype(BF)
            acc_ref[:, sl] = alpha * acc_ref[:, sl] + jnp.dot(va, e, preferred_element_type=F32)
            m_ref[:, sl] = m_new

    n_full = (t0 + tq + tk - 1) // tk - 1
    start = n_full % 2

    @pl.when(n_full == 0)
    def _():
        produce(0, sa_ref, ma_ref, True)

    @pl.when((n_full > 0) & (start == 0))
    def _():
        produce(0, sa_ref, ma_ref, False)

    @pl.when(n_full == 1)
    def _():
        produce(0, sb_ref, mb_ref, False)
        produce(1, sa_ref, ma_ref, True)
        consume(0, sb_ref, mb_ref)

    @pl.when((n_full > 1) & (start == 1))
    def _():
        produce(0, sb_ref, mb_ref, False)
        produce(1, sa_ref, ma_ref, False)
        consume(0, sb_ref, mb_ref)

    def sel_pair(jj, carry):
        ta = start + 2 * jj
        produce(ta + 1, sb_ref, mb_ref, False)
        consume(ta, sa_ref, ma_ref)
        produce(ta + 2, sa_ref, ma_ref, False)
        consume(ta + 1, sb_ref, mb_ref)
        return carry

    n_pairs = n_full // 2
    lax.fori_loop(0, n_pairs - 1, sel_pair, 0)

    @pl.when(n_pairs > 0)
    def _():
        ta = n_full - 2
        produce(ta + 1, sb_ref, mb_ref, False)
        consume(ta, sa_ref, ma_ref)
        produce(ta + 2, sa_ref, ma_ref, True)
        consume(ta + 1, sb_ref, mb_ref)

    consume(n_full, sa_ref, ma_ref)

    o_w = ow_ref[...]
    o_s = acc_ref[0:Dh, :] * (1.0 / acc_ref[Dh:Dh + 1, :])

    for h in range(H):
        sl = slice(h * tq, (h + 1) * tq)
        out = (gtT[3 * h:3 * h + 1, :] * o_c[:, sl] + gtT[3 * h + 1:3 * h + 2, :] * o_s[:, sl]
               + gtT[3 * h + 2:3 * h + 3, :] * o_w[:, sl])
        o_ref[:, h * Dh:(h + 1) * Dh] = out.T.astype(BF)


def _nsa_core(P, gates, cmp_kv, ovt, *, B, S, tq, tk):
    G, HPG, Dh = NSA_KV_HEADS, NSA_HPG, NSA_DH
    nq = S // tq
    ncp = cmp_kv.shape[3]
    n_sel = S // NSA_SEL_LEN
    n_top = min(NSA_SEL_TOPK, n_sel)
    assert S >= NSA_WINDOW + tq and NSA_WINDOW % tq == 0 and S % tk == 0 and tk % NSA_SEL_LEN == 0
    assert S % 512 == 0
    qcols = NSA_HEADS * Dh // LANE
    kvcols = G

    def kv_spec(slab):
        off = qcols + slab * kvcols
        return pl.BlockSpec((S, Dh), lambda b, g, i: (b, off + g))

    return pl.pallas_call(
        functools.partial(_nsa_kernel, tq=tq, tk=tk, n_top=n_top),
        grid=(B, G, nq),
        in_specs=[
            pl.BlockSpec((tq, HPG * Dh), lambda b, g, i: (b * nq + i, g)),
            pl.BlockSpec((tq, LANE), lambda b, g, i: (b * nq + i, g)),
            pl.BlockSpec((1, 1, 1, ncp, Dh), lambda b, g, i: (b, 0, g, 0, 0)),
            pl.BlockSpec((1, 1, 1, ncp, Dh), lambda b, g, i: (b, 1, g, 0, 0)),
            kv_spec(0), kv_spec(1), kv_spec(2), kv_spec(3),
            pl.BlockSpec(ovt.shape, lambda b, g, i: (0, 0)),
        ],
        out_specs=pl.BlockSpec((tq, HPG * Dh), lambda b, g, i: (b * nq + i, g)),
        out_shape=jax.ShapeDtypeStruct((B * S, NSA_HEADS * Dh), BF),
        scratch_shapes=[
            pltpu.VMEM((S, Dh + LANE), BF),
            pltpu.VMEM((Dh + SUM_ROWS, S), BF),
            pltpu.VMEM((Dh + SUM_ROWS, S), BF),
            pltpu.VMEM((1, HPG * tq), F32),
            pltpu.VMEM((Dh + SUM_ROWS, HPG * tq), F32),
            pltpu.VMEM((tk, HPG * tq), F32),
            pltpu.VMEM((tk, HPG * tq), F32),
            pltpu.VMEM((1, HPG * tq), F32),
            pltpu.VMEM((1, HPG * tq), F32),
            pltpu.VMEM((Dh, HPG * tq), F32),
        ],
        compiler_params=_params("parallel", "parallel", "arbitrary"),
        name="nsa_core",
    )(P, gates, cmp_kv, cmp_kv, P, P, P, P, ovt)


def _overlap_matrix_t(S, ncp):
    n_cmp = (S - NSA_CMP_LEN) // NSA_CMP_STRIDE + 1
    n_sel = S // NSA_SEL_LEN
    cs = np.arange(n_cmp) * NSA_CMP_STRIDE
    ss = np.arange(n_sel) * NSA_SEL_LEN
    ovl = ((cs[:, None] <= ss[None, :] + NSA_SEL_LEN - 1)
           & (cs[:, None] + NSA_CMP_LEN - 1 >= ss[None, :])).astype(np.float32)
    out = np.zeros((n_sel, ncp), np.float32)
    out[:, :n_cmp] = ovl.T
    return jnp.asarray(out, BF)


def _nsa_layer(h, gain, w_in, w_out, pos_k, w1_k, w2_k, pos_v, w1_v, w2_v, *, B, S):
    G, HPG, Dh = NSA_KV_HEADS, NSA_HPG, NSA_DH
    D = h.shape[1]
    qw, kvw = NSA_HEADS * Dh, G * Dh
    main = qw + 6 * kvw
    w_main = jnp.concatenate([w_in[:, :qw] * (Dh ** -0.5 * _LOG2E), w_in[:, qw + 2 * kvw:main]],
                             axis=1).astype(BF)
    wg = w_in[:, main:].reshape(D, G, 3 * HPG)
    wg = jnp.pad(wg, ((0, 0), (0, 0), (0, LANE - 3 * HPG))).reshape(D, G * LANE)
    w_aux = jnp.concatenate([wg, w_in[:, qw:qw + 2 * kvw]], axis=1).astype(BF)
    P, aux = _proj(h, gain, w_main, w_aux, tm=TM, tn=TN_PROJ)

    pos = jnp.stack([pos_k, pos_v])
    w1 = jnp.stack([w1_k, w1_v]).astype(BF)
    w2 = jnp.stack([w2_k, w2_v]).astype(BF)
    cmp_kv = _compress(aux, pos, w1, w2, B=B, S=S, col0=G)

    ovt = _overlap_matrix_t(S, S // NSA_CMP_STRIDE)
    o = _nsa_core(P, aux, cmp_kv, ovt, B=B, S=S, tq=TQ_NSA, tk=TK_NSA)
    return _outproj(o, w_out.astype(BF), h, tm=TM)


def _conv_kernel(u_ref, halo_ref, dw_ref, db_ref, lg_ref, lb_ref, w_ref, h_ref, o_ref,
                 buf_ref, sh_ref, y_ref, *, ts):
    C = u_ref.shape[-1]
    halo = halo_ref[...]
    buf_ref[0:CONV_HALO, :] = jnp.where(pl.program_id(1) > 0, halo, 0.0)
    buf_ref[CONV_HALO:CONV_HALO + ts, :] = u_ref[...]
    off = CONV_HALO - (CONV_WIDTH - 1)
    rs = 128
    sub = 8
    span = ts + CONV_HALO - sub

    def chan_body(c, carry):
        c0 = pl.multiple_of(c * LANE, LANE)
        for r in range(1, sub):
            sh_ref[r, 0:span, :] = buf_ref[pl.ds(r, span), pl.ds(c0, LANE)]
        for rr in range(ts // rs):
            acc = jnp.zeros((rs, LANE), F32) + db_ref[:, pl.ds(c0, LANE)]
            for k in range(CONV_WIDTH):
                r = (off + k) % sub
                base = rr * rs + off + k - r
                if r == 0:
                    win = buf_ref[pl.ds(base, rs), pl.ds(c0, LANE)]
                else:
                    win = sh_ref[r, pl.ds(base, rs), :]
                acc = acc + win * dw_ref[k:k + 1, pl.ds(c0, LANE)]
            y_ref[pl.ds(rr * rs, rs), pl.ds(c0, LANE)] = acc
        return carry

    lax.fori_loop(0, C // LANE, chan_body, 0)

    y = y_ref[...]
    mu = jnp.mean(y, axis=-1, keepdims=True)
    yc = y - mu
    var = jnp.mean(yc * yc, axis=-1, keepdims=True)
    z = yc * lax.rsqrt(var + LN_EPS) * lg_ref[...] + lb_ref[...]
    z = (z * _sigmoid(z)).astype(BF)
    o_ref[...] = h_ref[...] + jnp.dot(z, w_ref[...], preferred_element_type=F32)


def _conv_tail(u, dw, db, ln_g, ln_b, w_out, h, *, B, S, ts):
    M, C = u.shape
    D = w_out.shape[1]
    ns = S // ts
    hb = ts // CONV_HALO
    dwp = jnp.pad(dw, ((0, CONV_HALO - CONV_WIDTH), (0, 0)))
    return pl.pallas_call(
        functools.partial(_conv_kernel, ts=ts),
        grid=(B, ns),
        in_specs=[
            pl.BlockSpec((ts, C), lambda b, i: (b * ns + i, 0)),
            pl.BlockSpec((CONV_HALO, C), lambda b, i: (jnp.maximum((b * ns + i) * hb - 1, 0), 0)),
            pl.BlockSpec((CONV_HALO, C), lambda b, i: (0, 0)),
            pl.BlockSpec((1, C), lambda b, i: (0, 0)),
            pl.BlockSpec((1, C), lambda b, i: (0, 0)),
            pl.BlockSpec((1, C), lambda b, i: (0, 0)),
            pl.BlockSpec((C, D), lambda b, i: (0, 0)),
            pl.BlockSpec((ts, D), lambda b, i: (b * ns + i, 0)),
        ],
        out_specs=pl.BlockSpec((ts, D), lambda b, i: (b * ns + i, 0)),
        out_shape=jax.ShapeDtypeStruct((M, D), F32),
        scratch_shapes=[pltpu.VMEM((CONV_HALO + ts, C), F32), pltpu.VMEM((8, CONV_HALO + ts, LANE), F32),
                        pltpu.VMEM((ts, C), F32)],
        compiler_params=_params("parallel", "parallel"),
        name="conv_tail",
    )(u, u, dwp, db.reshape(1, C), ln_g.reshape(1, C), ln_b.reshape(1, C), w_out, h)


def _conv_layer(h, gain, w_in, dw, db, ln_g, ln_b, w_out, *, B, S):
    u = _glu_proj(h, gain, w_in.astype(BF), tm=TM, tn=TN_GLU)
    return _conv_tail(u, dw, db, ln_g, ln_b, w_out.astype(BF), h, B=B, S=S, ts=TS_SEQ)


def _gla_kernel(q_ref, k_ref, v_ref, r_ref, gz_ref, wgu_ref, bg_ref, ng_ref, o_ref, st_ref, *, ts, nh):
    C = GLA_CHUNK
    dk, dv = GLA_DK, GLA_DV

    @pl.when(pl.program_id(2) == 0)
    def _():
        st_ref[...] = jnp.zeros(st_ref.shape, F32)

    z = jnp.dot(gz_ref[...].astype(BF), wgu_ref[...], preferred_element_type=F32) + bg_ref[...]
    glog = (jnp.minimum(z, 0.0) - jnp.log(1.0 + jnp.exp(-jnp.abs(z)))) / GLA_GATE_NORM

    g1 = glog.astype(BF)
    r1 = glog - g1.astype(F32)
    g2 = r1.astype(BF)
    g3 = (r1 - g2.astype(F32)).astype(BF)
    row = lax.broadcasted_iota(jnp.int32, (ts, ts), 0)
    col = lax.broadcasted_iota(jnp.int32, (ts, ts), 1)
    same = jnp.right_shift(row, _CHUNK_SHIFT) == jnp.right_shift(col, _CHUNK_SHIFT)
    tri = jnp.where(same & (col <= row), 1.0, 0.0).astype(BF)
    full = jnp.where(same, 1.0, 0.0).astype(BF)

    def msum(mat):
        return (jnp.dot(mat, g1, preferred_element_type=F32)
                + jnp.dot(mat, g2, preferred_element_type=F32)
                + jnp.dot(mat, g3, preferred_element_type=F32))

    bcum = msum(tri)
    blast = msum(full)

    q = q_ref[...].astype(F32) * (dk ** -0.5)
    k = k_ref[...].astype(F32)
    q_dec = (q * jnp.exp(bcum)).astype(BF)
    k_intra = (k * jnp.exp(-bcum)).astype(BF)
    k_state = (k * jnp.exp(blast - bcum)).astype(BF)
    decay = jnp.exp(blast)
    causal = (lax.broadcasted_iota(jnp.int32, (C, C), 1) <= lax.broadcasted_iota(jnp.int32, (C, C), 0))

    outs = [[] for _ in range(nh)]
    for c in range(ts // C):
        sl = slice(c * C, (c + 1) * C)
        for hh in range(nh):
            ka = slice(hh * dk, (hh + 1) * dk)
            vc = v_ref[sl, hh * dv:(hh + 1) * dv]
            a = lax.dot_general(q_dec[sl, ka], k_intra[sl, ka], _NT, preferred_element_type=F32)
            a = jnp.where(causal, a, 0.0).astype(BF)
            st = st_ref[hh]
            o = (jnp.dot(a, vc, preferred_element_type=F32)
                 + lax.dot_general(q_dec[sl, ka], st.astype(BF), _NT, preferred_element_type=F32))
            outs[hh].append(o)
            st_ref[hh] = st * decay[c * C:c * C + 1, ka] + lax.dot_general(
                vc, k_state[sl, ka], _TN, preferred_element_type=F32)

    for hh in range(nh):
        o = jnp.concatenate(outs[hh], axis=0)
        o = o * lax.rsqrt(jnp.mean(o * o, axis=-1, keepdims=True) + NORM_EPS) * ng_ref[...]
        r = r_ref[:, hh * dv:(hh + 1) * dv].astype(F32)
        o_ref[:, hh * dv:(hh + 1) * dv] = (o * (r * _sigmoid(r))).astype(BF)


def _gla_core(P, gz, wgu, bg, ng, *, B, S, ts, nh):
    H, dk, dv = GLA_HEADS, GLA_DK, GLA_DV
    ns = S // ts
    hg = H // nh
    kq = hg
    vv = 2 * H * dk // (nh * dv)
    rr = vv + hg
    return pl.pallas_call(
        functools.partial(_gla_kernel, ts=ts, nh=nh),
        grid=(B, hg, ns),
        in_specs=[
            pl.BlockSpec((ts, nh * dk), lambda b, hh, i: (b * ns + i, hh)),
            pl.BlockSpec((ts, nh * dk), lambda b, hh, i: (b * ns + i, kq + hh)),
            pl.BlockSpec((ts, nh * dv), lambda b, hh, i: (b * ns + i, vv + hh)),
            pl.BlockSpec((ts, nh * dv), lambda b, hh, i: (b * ns + i, rr + hh)),
            pl.BlockSpec((ts, LANE), lambda b, hh, i: (b * ns + i, 0)),
            pl.BlockSpec((LANE, nh * dk), lambda b, hh, i: (0, hh)),
            pl.BlockSpec((1, nh * dk), lambda b, hh, i: (0, hh)),
            pl.BlockSpec((1, dv), lambda b, hh, i: (0, 0)),
        ],
        out_specs=pl.BlockSpec((ts, nh * dv), lambda b, hh, i: (b * ns + i, hh)),
        out_shape=jax.ShapeDtypeStruct((B * S, H * dv), BF),
        scratch_shapes=[pltpu.VMEM((nh, dv, dk), F32)],
        compiler_params=_params("parallel", "parallel", "arbitrary"),
        name="gla_core",
    )(P, P, P, P, gz, wgu, bg.reshape(1, H * dk), ng.reshape(1, dv))


def _gla_layer(h, gain, w_in, w_gate_up, b_gate, norm_g, w_out, *, B, S):
    H, dk, dv = GLA_HEADS, GLA_DK, GLA_DV
    main = 2 * H * dk + 2 * H * dv
    w_main = w_in[:, :main].astype(BF)
    w_gz = jnp.pad(w_in[:, main:], ((0, 0), (0, LANE - GLA_GATE_RANK))).astype(BF)
    P, gz = _proj(h, gain, w_main, w_gz, tm=TM, tn=TN_PROJ)
    wgu = jnp.pad(w_gate_up, ((0, LANE - GLA_GATE_RANK), (0, 0))).astype(BF)
    o = _gla_core(P, gz, wgu, b_gate, norm_g, B=B, S=S, ts=TS_SEQ, nh=GLA_HEADS)
    return _outproj(o, w_out.astype(BF), h, tm=TM)


def kernel(x, p, norm_mix, norm_ffn, norm_ple, norm_final, ffn_w1, ffn_w2, ple_w_proj, ple_w_gate,
           nsa_w_in, nsa_w_out, nsa_cmp_pos_k, nsa_cmp_w1_k, nsa_cmp_w2_k, nsa_cmp_pos_v,
           nsa_cmp_w1_v, nsa_cmp_w2_v, conv_w_in, conv_dw, conv_db, conv_ln_g, conv_ln_b,
           conv_w_out, gla_w_in, gla_w_gate_up, gla_b_gate, gla_norm_g, gla_w_out):
    B, S, D = x.shape
    depth = p.shape[0]
    M = B * S
    h = x.reshape(M, D)
    p2 = p.reshape(depth, M, p.shape[-1])
    ffn_w1, ffn_w2 = ffn_w1.astype(BF), ffn_w2.astype(BF)
    ple_w_gate, ple_w_proj = ple_w_gate.astype(BF), ple_w_proj.astype(BF)
    for i in range(depth):
        m, j = i % 3, i // 3
        if m == 0:
            h = _nsa_layer(h, norm_mix[i], nsa_w_in[j], nsa_w_out[j], nsa_cmp_pos_k[j],
                           nsa_cmp_w1_k[j], nsa_cmp_w2_k[j], nsa_cmp_pos_v[j], nsa_cmp_w1_v[j],
                           nsa_cmp_w2_v[j], B=B, S=S)
        elif m == 1:
            h = _conv_layer(h, norm_mix[i], conv_w_in[j], conv_dw[j], conv_db[j], conv_ln_g[j],
                            conv_ln_b[j], conv_w_out[j], B=B, S=S)
        else:
            h = _gla_layer(h, norm_mix[i], gla_w_in[j], gla_w_gate_up[j], gla_b_gate[j],
                           gla_norm_g[j], gla_w_out[j], B=B, S=S)
        h = _ffn(h, norm_ffn[i], ffn_w1, ffn_w2, i, tm=TM_FFN, tf=TF_FFN)
        h = _ple(h, p2, norm_ple[i], ple_w_gate, ple_w_proj, i, norm_final,
                 final=(i == depth - 1), tm=TM_PLE)
    return h.reshape(B, S, D)
```

```python
import functools

import numpy as np
import jax
import jax.numpy as jnp
from jax import lax
from jax.experimental import pallas as pl
from jax.experimental.pallas import tpu as pltpu

BF = jnp.bfloat16
F32 = jnp.float32

NORM_EPS = 1e-6
LN_EPS = 1e-5
NEG = -1e30
LANE = 128
VMEM_LIMIT = 56 * 1024 * 1024

NSA_HEADS = 16
NSA_KV_HEADS = 4
NSA_HPG = NSA_HEADS // NSA_KV_HEADS
NSA_DH = 128
NSA_CMP_LEN = 32
NSA_CMP_STRIDE = 16
NSA_SEL_LEN = 64
NSA_SEL_TOPK = 16
NSA_SEL_LOCAL = 2
NSA_WINDOW = 512
NSA_FORCE = 1e9

SUM_ROWS = 16

CONV_WIDTH = 31
CONV_HALO = 32

GLA_HEADS = 4
GLA_DK = 256
GLA_DV = 512
GLA_GATE_RANK = 16
GLA_GATE_NORM = 16.0
GLA_CHUNK = 64

TM = 512
TM_PROJ = 1024
TM_FFN = 1024
TM_PLE = 256
TN_PROJ = 1024
TN_AUX = 512
TN_GLU = 512
TF_FFN = 512
TQ_NSA = 256
TK_NSA = 512
TS_SEQ = 256

_LOG2E = 1.4426950408889634
_SEL_SHIFT = NSA_SEL_LEN.bit_length() - 1
_CHUNK_SHIFT = GLA_CHUNK.bit_length() - 1

_NT = (((1,), (1,)), ((), ()))
_TN = (((0,), (0,)), ((), ()))


def _params(*sem):
    return pltpu.CompilerParams(dimension_semantics=sem, vmem_limit_bytes=VMEM_LIMIT)


def _sigmoid(x):
    return 1.0 / (1.0 + jnp.exp(-x))


def _rms_bf16(x, g):
    ms = jnp.mean(x * x, axis=-1, keepdims=True)
    return (x * lax.rsqrt(ms + NORM_EPS) * g).astype(BF)


def _proj_kernel(x_ref, g_ref, w_ref, wa_ref, o_ref, oa_ref, xn_ref, *, nj):
    j = pl.program_id(1)

    @pl.when(j == 0)
    def _():
        xn_ref[...] = _rms_bf16(x_ref[...], g_ref[...])

    @pl.when(j < nj)
    def _():
        o_ref[...] = jnp.dot(xn_ref[...], w_ref[...], preferred_element_type=F32).astype(o_ref.dtype)

    @pl.when(j >= nj)
    def _():
        oa_ref[...] = jnp.dot(xn_ref[...], wa_ref[...], preferred_element_type=F32)


def _proj(x, gain, w, w_aux, *, tm, tn, tna):
    M, K = x.shape
    N = w.shape[1]
    Na = w_aux.shape[1]
    nj, nja = N // tn, Na // tna
    return pl.pallas_call(
        functools.partial(_proj_kernel, nj=nj),
        grid=(M // tm, nj + nja),
        in_specs=[
            pl.BlockSpec((tm, K), lambda i, j: (i, 0)),
            pl.BlockSpec((1, K), lambda i, j: (0, 0)),
            pl.BlockSpec((K, tn), lambda i, j: (0, jnp.minimum(j, nj - 1))),
            pl.BlockSpec((K, tna), lambda i, j: (0, jnp.maximum(j - nj, 0))),
        ],
        out_specs=[
            pl.BlockSpec((tm, tn), lambda i, j: (i, jnp.minimum(j, nj - 1))),
            pl.BlockSpec((tm, tna), lambda i, j: (i, jnp.maximum(j - nj, 0))),
        ],
        out_shape=[jax.ShapeDtypeStruct((M, N), BF), jax.ShapeDtypeStruct((M, Na), F32)],
        scratch_shapes=[pltpu.VMEM((tm, K), BF)],
        compiler_params=_params("parallel", "arbitrary"),
        name="proj",
    )(x, gain.reshape(1, K), w, w_aux)


def _glu_kernel(x_ref, g_ref, wa_ref, wg_ref, o_ref, xn_ref):
    @pl.when(pl.program_id(1) == 0)
    def _():
        xn_ref[...] = _rms_bf16(x_ref[...], g_ref[...])

    xn = xn_ref[...]
    a = jnp.dot(xn, wa_ref[...], preferred_element_type=F32)
    g = jnp.dot(xn, wg_ref[...], preferred_element_type=F32)
    o_ref[...] = a * _sigmoid(g)


def _glu_proj(x, gain, w, *, tm, tn):
    M, K = x.shape
    C = w.shape[1] // 2
    nj = C // tn
    return pl.pallas_call(
        _glu_kernel,
        grid=(M // tm, nj),
        in_specs=[
            pl.BlockSpec((tm, K), lambda i, j: (i, 0)),
            pl.BlockSpec((1, K), lambda i, j: (0, 0)),
            pl.BlockSpec((K, tn), lambda i, j: (0, j)),
            pl.BlockSpec((K, tn), lambda i, j: (0, j + nj)),
        ],
        out_specs=pl.BlockSpec((tm, tn), lambda i, j: (i, j)),
        out_shape=jax.ShapeDtypeStruct((M, C), F32),
        scratch_shapes=[pltpu.VMEM((tm, K), BF)],
        compiler_params=_params("parallel", "arbitrary"),
        name="glu_proj",
    )(x, gain.reshape(1, K), w, w)


def _outproj_kernel(a_ref, w_ref, h_ref, o_ref):
    o_ref[...] = h_ref[...] + jnp.dot(a_ref[...], w_ref[...], preferred_element_type=F32)


def _outproj(a, w, h, *, tm):
    M, K = a.shape
    N = w.shape[1]
    return pl.pallas_call(
        _outproj_kernel,
        grid=(M // tm,),
        in_specs=[
            pl.BlockSpec((tm, K), lambda i: (i, 0)),
            pl.BlockSpec((K, N), lambda i: (0, 0)),
            pl.BlockSpec((tm, N), lambda i: (i, 0)),
        ],
        out_specs=pl.BlockSpec((tm, N), lambda i: (i, 0)),
        out_shape=jax.ShapeDtypeStruct((M, N), F32),
        compiler_params=_params("parallel"),
        name="outproj",
    )(a, w, h)


def _ffn_kernel(x_ref, g_ref, w1_ref, w2_ref, o_ref, xn_ref):
    @pl.when(pl.program_id(1) == 0)
    def _():
        x = x_ref[...]
        xn_ref[...] = _rms_bf16(x, g_ref[...])
        o_ref[...] = x

    a = jnp.dot(xn_ref[...], w1_ref[...], preferred_element_type=F32)
    a = jnp.square(jnp.maximum(a, 0.0)).astype(BF)
    o_ref[...] += jnp.dot(a, w2_ref[...], preferred_element_type=F32)


def _ffn(x, gain, w1, w2, layer, *, tm, tf):
    M, K = x.shape
    Fd = w1.shape[2]
    return pl.pallas_call(
        _ffn_kernel,
        grid=(M // tm, Fd // tf),
        in_specs=[
            pl.BlockSpec((tm, K), lambda i, j: (i, 0)),
            pl.BlockSpec((1, K), lambda i, j: (0, 0)),
            pl.BlockSpec((None, K, tf), lambda i, j: (layer, 0, j)),
            pl.BlockSpec((None, tf, K), lambda i, j: (layer, j, 0)),
        ],
        out_specs=pl.BlockSpec((tm, K), lambda i, j: (i, 0)),
        out_shape=jax.ShapeDtypeStruct((M, K), F32),
        scratch_shapes=[pltpu.VMEM((tm, K), BF)],
        compiler_params=_params("parallel", "arbitrary"),
        name="ffn",
    )(x, gain.reshape(1, K), w1, w2)


def _ple_kernel(h_ref, p_ref, g_ref, wg_ref, wp_ref, gf_ref, o_ref, *, final):
    h = h_ref[...]
    gate = _sigmoid(jnp.dot(_rms_bf16(h, g_ref[...]), wg_ref[...], preferred_element_type=F32))
    proj = jnp.dot(p_ref[...].astype(BF), wp_ref[...], preferred_element_type=F32)
    out = h + proj * gate
    if final:
        ms = jnp.mean(out * out, axis=-1, keepdims=True)
        out = out * lax.rsqrt(ms + NORM_EPS) * gf_ref[...]
    o_ref[...] = out


def _ple(h, p, gain, w_gate, w_proj, layer, gain_final, *, final, tm):
    M, K = h.shape
    Pd = p.shape[2]
    return pl.pallas_call(
        functools.partial(_ple_kernel, final=final),
        grid=(M // tm,),
        in_specs=[
            pl.BlockSpec((tm, K), lambda i: (i, 0)),
            pl.BlockSpec((None, tm, Pd), lambda i: (layer, i, 0)),
            pl.BlockSpec((1, K), lambda i: (0, 0)),
            pl.BlockSpec((None, K, K), lambda i: (layer, 0, 0)),
            pl.BlockSpec((None, Pd, K), lambda i: (layer, 0, 0)),
            pl.BlockSpec((1, K), lambda i: (0, 0)),
        ],
        out_specs=pl.BlockSpec((tm, K), lambda i: (i, 0)),
        out_shape=jax.ShapeDtypeStruct((M, K), F32),
        compiler_params=_params("parallel"),
        name="ple",
    )(h, p, gain.reshape(1, K), w_gate, w_proj, gain_final.reshape(1, K))


def _compress_kernel(x_ref, pos_ref, w1_ref, w2_ref, o_ref):
    Dh = x_ref.shape[-1]
    R = x_ref.shape[0] // NSA_CMP_STRIDE
    nl = NSA_CMP_STRIDE
    a = jnp.zeros((R, w1_ref.shape[-1]), F32)
    b = jnp.zeros((R, w1_ref.shape[-1]), F32)
    for l in range(nl):
        xl = x_ref[pl.ds(l, R, stride=nl), :]
        a = a + jnp.dot((xl + pos_ref[0, l:l + 1, :]).astype(BF),
                        w1_ref[0, l * Dh:(l + 1) * Dh, :], preferred_element_type=F32)
        b = b + jnp.dot((xl + pos_ref[0, nl + l:nl + l + 1, :]).astype(BF),
                        w1_ref[0, (nl + l) * Dh:(nl + l + 1) * Dh, :], preferred_element_type=F32)
    hid = a + pltpu.roll(b, R - 1, 0)
    hid = hid * _sigmoid(hid)
    out = jnp.dot(hid.astype(BF), w2_ref[0], preferred_element_type=F32)
    row = lax.broadcasted_iota(jnp.int32, out.shape, 0)
    o_ref[0, 0, 0] = jnp.where(row < R - 1, out, 0.0).astype(BF)


def _compress(aux, pos, w1, w2, *, B, S, col0):
    G, Dh = NSA_KV_HEADS, NSA_DH
    R = S // NSA_CMP_STRIDE
    return pl.pallas_call(
        _compress_kernel,
        grid=(B, 2, G),
        in_specs=[
            pl.BlockSpec((S, Dh), lambda b, c, g: (b, col0 + c * G + g)),
            pl.BlockSpec((1,) + pos.shape[1:], lambda b, c, g: (c, 0, 0)),
            pl.BlockSpec((1,) + w1.shape[1:], lambda b, c, g: (c, 0, 0)),
            pl.BlockSpec((1,) + w2.shape[1:], lambda b, c, g: (c, 0, 0)),
        ],
        out_specs=pl.BlockSpec((1, 1, 1, R, Dh), lambda b, c, g: (b, c, g, 0, 0)),
        out_shape=jax.ShapeDtypeStruct((B, 2, G, R, Dh), BF),
        compiler_params=_params("parallel", "parallel", "parallel"),
        name="nsa_compress",
    )(aux, pos, w1, w2)


def _nsa_kernel(q_ref, gt_ref, kc_ref, vc_ref, ks_ref, vs_ref, kw_ref, vw_ref, ovt_ref, o_ref,
                ksa_ref, vst_ref, vwt_ref, m_ref, acc_ref, sa_ref, sb_ref, ma_ref, mb_ref, ow_ref, *, tq, tk, n_top):
    Dh = NSA_DH
    i = pl.program_id(2)
    t0 = i * tq
    S = ks_ref.shape[0]
    tc = 512

    @pl.when(i == 0)
    def _():
        lane = lax.broadcasted_iota(jnp.int32, (tc, LANE), 1)
        krow = lax.broadcasted_iota(jnp.int32, (tc, LANE), 0)
        ones_rows = jnp.where(lax.broadcasted_iota(jnp.int32, (SUM_ROWS, tc), 0) == 0, 1.0, 0.0).astype(BF)
        for c in range(S // tc):
            sl = slice(c * tc, (c + 1) * tc)
            vst_ref[0:Dh, sl] = vs_ref[sl, :].astype(F32).T.astype(BF)
            vst_ref[Dh:Dh + SUM_ROWS, sl] = ones_rows
            vwt_ref[0:Dh, sl] = vw_ref[sl, :].astype(F32).T.astype(BF)
            vwt_ref[Dh:Dh + SUM_ROWS, sl] = ones_rows
            ksa_ref[sl, 0:Dh] = ks_ref[sl, :]
            ksa_ref[sl, Dh:Dh + LANE] = jnp.where(
                jnp.right_shift(c * tc + krow, _SEL_SHIFT) == lane, 1.0, 0.0).astype(BF)

    H = NSA_HPG
    tcol = t0 + lax.broadcasted_iota(jnp.int32, (1, tq), 1)
    qst = jnp.concatenate([q_ref[:, h * Dh:(h + 1) * Dh] for h in range(H)], axis=0)
    gtT = _sigmoid(gt_ref[...]).T

    def heads(x):
        return jnp.concatenate([x] * H, axis=1)

    kc = kc_ref[0, 0, 0]
    vcT = vc_ref[0, 0, 0].astype(F32).T.astype(BF)
    ncp = kc.shape[0]
    n_row = lax.broadcasted_iota(jnp.int32, (ncp, tq), 0)
    valid_c = heads((n_row * NSA_CMP_STRIDE + (NSA_CMP_LEN - 1)) <= tcol)
    s = lax.dot_general(kc, qst, _NT, preferred_element_type=F32)
    s = jnp.where(valid_c, s, NEG)
    e = jnp.exp2(s - jnp.max(s, axis=0, keepdims=True))
    den = jnp.sum(e, axis=0, keepdims=True)
    p = jnp.where(valid_c, e, 0.0) * (1.0 / den)
    o_c = jnp.dot(vcT, p.astype(BF), preferred_element_type=F32)
    psum = p[:, 0:tq]
    for h in range(1, H):
        psum = psum + p[:, h * tq:(h + 1) * tq]

    hw = (H // 2) * tq

    wlen = NSA_WINDOW + tq
    ws = pl.multiple_of(jnp.maximum(t0 - NSA_WINDOW, 0), tq)
    kwt = kw_ref[pl.ds(ws, wlen), :]
    vwT = vwt_ref[:, pl.ds(ws, wlen)]
    kpos_w = ws + lax.broadcasted_iota(jnp.int32, (wlen, tq), 0)
    ok_w = (kpos_w <= tcol) & (kpos_w > tcol - NSA_WINDOW)
    bias_w = jnp.where(ok_w, 0.0, NEG)
    bias_w2 = jnp.concatenate([bias_w] * (H // 2), axis=1)
    sw = [lax.dot_general(kwt, qst[c * hw:(c + 1) * hw], _NT, preferred_element_type=F32) + bias_w2
          for c in range(2)]
    o_w = []
    for c in range(2):
        e = jnp.exp2(sw[c] - jnp.max(sw[c], axis=0, keepdims=True)).astype(BF)
        ow = jnp.dot(vwT, e, preferred_element_type=F32)
        o_w.append(ow[0:Dh] * (1.0 / ow[Dh:Dh + 1]))
    ow_ref[...] = jnp.concatenate(o_w, axis=1)

    ovt = ovt_ref[...]
    p_hi = psum.astype(BF)
    p_lo = (psum - p_hi.astype(F32)).astype(BF)
    imp = (jnp.dot(ovt, p_hi, preferred_element_type=F32)
           + jnp.dot(ovt, p_lo, preferred_element_type=F32))
    n_sel = imp.shape[0]
    m_idx = lax.broadcasted_iota(jnp.int32, imp.shape, 0)
    m_idx_f = m_idx.astype(F32)
    diff = jnp.right_shift(tcol, _SEL_SHIFT) - m_idx
    forced = (m_idx == 0) | ((diff >= 0) & (diff < NSA_SEL_LOCAL))
    member = jnp.where(forced, 1.0, 0.0)
    work = jnp.where(forced, -3e38, jnp.where(diff >= 0, imp, -NSA_FORCE))
    for _ in range(n_top - 1 - NSA_SEL_LOCAL):
        mx = jnp.max(work, axis=0, keepdims=True)
        first = jnp.min(jnp.where(work == mx, m_idx_f, 1e6), axis=0, keepdims=True)
        pick = m_idx_f == first
        member = jnp.where(pick, 1.0, member)
        work = jnp.where(pick, -3e38, work)
    sel_bias = jnp.where((member > 0.5) & (diff >= 0), 0.0, NEG)
    sel_bias = jnp.concatenate([sel_bias, jnp.zeros((LANE - n_sel, tq), F32)], axis=0)
    sb = sel_bias.T.astype(BF)
    q_aug = jnp.concatenate([qst, jnp.concatenate([sb] * H, axis=0)], axis=1)

    m_ref[...] = jnp.full(m_ref.shape, NEG, F32)
    acc_ref[...] = jnp.zeros(acc_ref.shape, F32)
    key_row = lax.broadcasted_iota(jnp.int32, (tk, tq), 0)

    def produce(j, s_ref, mt_ref, diagonal):
        k0 = pl.multiple_of(j * tk, tk)
        s = lax.dot_general(ksa_ref[pl.ds(k0, tk), :], q_aug, _NT, preferred_element_type=F32)
        if diagonal:
            s = jnp.where(heads((k0 + key_row) <= tcol), s, NEG)
        s_ref[...] = s
        mt_ref[...] = jnp.max(s, axis=0, keepdims=True)

    def consume(j, s_ref, mt_ref):
        k0 = pl.multiple_of(j * tk, tk)
        va = vst_ref[:, pl.ds(k0, tk)]
        for c in range(2):
            sl = slice(c * hw, (c + 1) * hw)
            m_prev = m_ref[:, sl]
            m_new = jnp.maximum(m_prev, mt_ref[:, sl])
            alpha = jnp.exp2(m_prev - m_new)
            e = jnp.exp2(s_ref[:, sl] - m_new).astype(BF)
            acc_ref[:, sl] = alpha * acc_ref[:, sl] + jnp.dot(va, e, preferred_element_type=F32)
            m_ref[:, sl] = m_new

    n_full = (t0 + tq + tk - 1) // tk - 1
    start = n_full % 2

    @pl.when(n_full == 0)
    def _():
        produce(0, sa_ref, ma_ref, True)

    @pl.when((n_full > 0) & (start == 0))
    def _():
        produce(0, sa_ref, ma_ref, False)

    @pl.when(n_full == 1)
    def _():
        produce(0, sb_ref, mb_ref, False)
        produce(1, sa_ref, ma_ref, True)
        consume(0, sb_ref, mb_ref)

    @pl.when((n_full > 1) & (start == 1))
    def _():
        produce(0, sb_ref, mb_ref, False)
        produce(1, sa_ref, ma_ref, False)
        consume(0, sb_ref, mb_ref)

    def sel_pair(jj, carry):
        ta = start + 2 * jj
        produce(ta + 1, sb_ref, mb_ref, False)
        consume(ta, sa_ref, ma_ref)
        produce(ta + 2, sa_ref, ma_ref, False)
        consume(ta + 1, sb_ref, mb_ref)
        return carry

    n_pairs = n_full // 2
    lax.fori_loop(0, n_pairs - 1, sel_pair, 0)

    @pl.when(n_pairs > 0)
    def _():
        ta = n_full - 2
        produce(ta + 1, sb_ref, mb_ref, False)
        consume(ta, sa_ref, ma_ref)
        produce(ta + 2, sa_ref, ma_ref, True)
        consume(ta + 1, sb_ref, mb_ref)

    consume(n_full, sa_ref, ma_ref)

    o_w = ow_ref[...]
    o_s = acc_ref[0:Dh, :] * (1.0 / acc_ref[Dh:Dh + 1, :])

    for h in range(H):
        sl = slice(h * tq, (h + 1) * tq)
        out = (gtT[3 * h:3 * h + 1, :] * o_c[:, sl] + gtT[3 * h + 1:3 * h + 2, :] * o_s[:, sl]
               + gtT[3 * h + 2:3 * h + 3, :] * o_w[:, sl])
        o_ref[:, h * Dh:(h + 1) * Dh] = out.T.astype(BF)


def _nsa_core(P, gates, cmp_kv, ovt, *, B, S, tq, tk):
    G, HPG, Dh = NSA_KV_HEADS, NSA_HPG, NSA_DH
    nq = S // tq
    ncp = cmp_kv.shape[3]
    n_sel = S // NSA_SEL_LEN
    n_top = min(NSA_SEL_TOPK, n_sel)
    assert S >= NSA_WINDOW + tq and NSA_WINDOW % tq == 0 and S % tk == 0 and tk % NSA_SEL_LEN == 0
    assert S % 512 == 0
    qcols = NSA_HEADS * Dh // LANE
    kvcols = G

    def kv_spec(slab):
        off = qcols + slab * kvcols
        return pl.BlockSpec((S, Dh), lambda b, g, i: (b, off + g))

    return pl.pallas_call(
        functools.partial(_nsa_kernel, tq=tq, tk=tk, n_top=n_top),
        grid=(B, G, nq),
        in_specs=[
            pl.BlockSpec((tq, HPG * Dh), lambda b, g, i: (b * nq + i, g)),
            pl.BlockSpec((tq, LANE), lambda b, g, i: (b * nq + i, g)),
            pl.BlockSpec((1, 1, 1, ncp, Dh), lambda b, g, i: (b, 0, g, 0, 0)),
            pl.BlockSpec((1, 1, 1, ncp, Dh), lambda b, g, i: (b, 1, g, 0, 0)),
            kv_spec(0), kv_spec(1), kv_spec(2), kv_spec(3),
            pl.BlockSpec(ovt.shape, lambda b, g, i: (0, 0)),
        ],
        out_specs=pl.BlockSpec((tq, HPG * Dh), lambda b, g, i: (b * nq + i, g)),
        out_shape=jax.ShapeDtypeStruct((B * S, NSA_HEADS * Dh), BF),
        scratch_shapes=[
            pltpu.VMEM((S, Dh + LANE), BF),
            pltpu.VMEM((Dh + SUM_ROWS, S), BF),
            pltpu.VMEM((Dh + SUM_ROWS, S), BF),
            pltpu.VMEM((1, HPG * tq), F32),
            pltpu.VMEM((Dh + SUM_ROWS, HPG * tq), F32),
            pltpu.VMEM((tk, HPG * tq), F32),
            pltpu.VMEM((tk, HPG * tq), F32),
            pltpu.VMEM((1, HPG * tq), F32),
            pltpu.VMEM((1, HPG * tq), F32),
            pltpu.VMEM((Dh, HPG * tq), F32),
        ],
        compiler_params=_params("parallel", "parallel", "arbitrary"),
        name="nsa_core",
    )(P, gates, cmp_kv, cmp_kv, P, P, P, P, ovt)


def _overlap_matrix_t(S, ncp):
    n_cmp = (S - NSA_CMP_LEN) // NSA_CMP_STRIDE + 1
    n_sel = S // NSA_SEL_LEN
    cs = np.arange(n_cmp) * NSA_CMP_STRIDE
    ss = np.arange(n_sel) * NSA_SEL_LEN
    ovl = ((cs[:, None] <= ss[None, :] + NSA_SEL_LEN - 1)
           & (cs[:, None] + NSA_CMP_LEN - 1 >= ss[None, :])).astype(np.float32)
    out = np.zeros((n_sel, ncp), np.float32)
    out[:, :n_cmp] = ovl.T
    return jnp.asarray(out, BF)


def _nsa_layer(h, gain, w_in, w_out, pos_k, w1_k, w2_k, pos_v, w1_v, w2_v, *, B, S):
    G, HPG, Dh = NSA_KV_HEADS, NSA_HPG, NSA_DH
    D = h.shape[1]
    qw, kvw = NSA_HEADS * Dh, G * Dh
    main = qw + 6 * kvw
    w_main = jnp.concatenate([w_in[:, :qw] * (Dh ** -0.5 * _LOG2E), w_in[:, qw + 2 * kvw:main]],
                             axis=1).astype(BF)
    wg = w_in[:, main:].reshape(D, G, 3 * HPG)
    wg = jnp.pad(wg, ((0, 0), (0, 0), (0, LANE - 3 * HPG))).reshape(D, G * LANE)
    w_aux = jnp.concatenate([wg, w_in[:, qw:qw + 2 * kvw]], axis=1).astype(BF)
    P, aux = _proj(h, gain, w_main, w_aux, tm=TM_PROJ, tn=TN_PROJ, tna=TN_AUX)

    pos = jnp.stack([pos_k, pos_v])
    w1 = jnp.stack([w1_k, w1_v]).astype(BF)
    w2 = jnp.stack([w2_k, w2_v]).astype(BF)
    cmp_kv = _compress(aux, pos, w1, w2, B=B, S=S, col0=G)

    ovt = _overlap_matrix_t(S, S // NSA_CMP_STRIDE)
    o = _nsa_core(P, aux, cmp_kv, ovt, B=B, S=S, tq=TQ_NSA, tk=TK_NSA)
    return _outproj(o, w_out.astype(BF), h, tm=TM)


def _conv_kernel(u_ref, halo_ref, dw_ref, db_ref, lg_ref, lb_ref, w_ref, h_ref, o_ref,
                 buf_ref, sh_ref, y_ref, *, ts):
    C = u_ref.shape[-1]
    halo = halo_ref[...]
    buf_ref[0:CONV_HALO, :] = jnp.where(pl.program_id(1) > 0, halo, 0.0)
    buf_ref[CONV_HALO:CONV_HALO + ts, :] = u_ref[...]
    off = CONV_HALO - (CONV_WIDTH - 1)
    rs = 128
    sub = 8
    span = ts + CONV_HALO - sub

    def chan_body(c, carry):
        c0 = pl.multiple_of(c * LANE, LANE)
        for r in range(1, sub):
            sh_ref[r, 0:span, :] = buf_ref[pl.ds(r, span), pl.ds(c0, LANE)]
        for rr in range(ts // rs):
            acc = jnp.zeros((rs, LANE), F32) + db_ref[:, pl.ds(c0, LANE)]
            for k in range(CONV_WIDTH):
                r = (off + k) % sub
                base = rr * rs + off + k - r
                if r == 0:
                    win = buf_ref[pl.ds(base, rs), pl.ds(c0, LANE)]
                else:
                    win = sh_ref[r, pl.ds(base, rs), :]
                acc = acc + win * dw_ref[k:k + 1, pl.ds(c0, LANE)]
            y_ref[pl.ds(rr * rs, rs), pl.ds(c0, LANE)] = acc
        return carry

    lax.fori_loop(0, C // LANE, chan_body, 0)

    y = y_ref[...]
    mu = jnp.mean(y, axis=-1, keepdims=True)
    yc = y - mu
    var = jnp.mean(yc * yc, axis=-1, keepdims=True)
    z = yc * lax.rsqrt(var + LN_EPS) * lg_ref[...] + lb_ref[...]
    z = (z * _sigmoid(z)).astype(BF)
    o_ref[...] = h_ref[...] + jnp.dot(z, w_ref[...], preferred_element_type=F32)


def _conv_tail(u, dw, db, ln_g, ln_b, w_out, h, *, B, S, ts):
    M, C = u.shape
    D = w_out.shape[1]
    ns = S // ts
    hb = ts // CONV_HALO
    dwp = jnp.pad(dw, ((0, CONV_HALO - CONV_WIDTH), (0, 0)))
    return pl.pallas_call(
        functools.partial(_conv_kernel, ts=ts),
        grid=(B, ns),
        in_specs=[
            pl.BlockSpec((ts, C), lambda b, i: (b * ns + i, 0)),
            pl.BlockSpec((CONV_HALO, C), lambda b, i: (jnp.maximum((b * ns + i) * hb - 1, 0), 0)),
            pl.BlockSpec((CONV_HALO, C), lambda b, i: (0, 0)),
            pl.BlockSpec((1, C), lambda b, i: (0, 0)),
            pl.BlockSpec((1, C), lambda b, i: (0, 0)),
            pl.BlockSpec((1, C), lambda b, i: (0, 0)),
            pl.BlockSpec((C, D), lambda b, i: (0, 0)),
            pl.BlockSpec((ts, D), lambda b, i: (b * ns + i, 0)),
        ],
        out_specs=pl.BlockSpec((ts, D), lambda b, i: (b * ns + i, 0)),
        out_shape=jax.ShapeDtypeStruct((M, D), F32),
        scratch_shapes=[pltpu.VMEM((CONV_HALO + ts, C), F32), pltpu.VMEM((8, CONV_HALO + ts, LANE), F32),
                        pltpu.VMEM((ts, C), F32)],
        compiler_params=_params("parallel", "parallel"),
        name="conv_tail",
    )(u, u, dwp, db.reshape(1, C), ln_g.reshape(1, C), ln_b.reshape(1, C), w_out, h)


def _conv_layer(h, gain, w_in, dw, db, ln_g, ln_b, w_out, *, B, S):
    u = _glu_proj(h, gain, w_in.astype(BF), tm=TM_PROJ, tn=TN_GLU)
    return _conv_tail(u, dw, db, ln_g, ln_b, w_out.astype(BF), h, B=B, S=S, ts=TS_SEQ)


def _gla_kernel(q_ref, k_ref, v_ref, r_ref, gz_ref, wgu_ref, bg_ref, ng_ref, o_ref, st_ref, *, ts, nh):
    C = GLA_CHUNK
    dk, dv = GLA_DK, GLA_DV

    @pl.when(pl.program_id(2) == 0)
    def _():
        st_ref[...] = jnp.zeros(st_ref.shape, F32)

    z = jnp.dot(gz_ref[...].astype(BF), wgu_ref[...], preferred_element_type=F32) + bg_ref[...]
    glog = (jnp.minimum(z, 0.0) - jnp.log(1.0 + jnp.exp(-jnp.abs(z)))) / GLA_GATE_NORM

    g1 = glog.astype(BF)
    r1 = glog - g1.astype(F32)
    g2 = r1.astype(BF)
    g3 = (r1 - g2.astype(F32)).astype(BF)
    row = lax.broadcasted_iota(jnp.int32, (ts, ts), 0)
    col = lax.broadcasted_iota(jnp.int32, (ts, ts), 1)
    same = jnp.right_shift(row, _CHUNK_SHIFT) == jnp.right_shift(col, _CHUNK_SHIFT)
    tri = jnp.where(same & (col <= row), 1.0, 0.0).astype(BF)
    full = jnp.where(same, 1.0, 0.0).astype(BF)

    def msum(mat):
        return (jnp.dot(mat, g1, preferred_element_type=F32)
                + jnp.dot(mat, g2, preferred_element_type=F32)
                + jnp.dot(mat, g3, preferred_element_type=F32))

    bcum = msum(tri)
    blast = msum(full)

    q = q_ref[...].astype(F32) * (dk ** -0.5)
    k = k_ref[...].astype(F32)
    q_dec = (q * jnp.exp(bcum)).astype(BF)
    k_intra = (k * jnp.exp(-bcum)).astype(BF)
    k_state = (k * jnp.exp(blast - bcum)).astype(BF)
    decay = jnp.exp(blast)
    causal = (lax.broadcasted_iota(jnp.int32, (C, C), 1) <= lax.broadcasted_iota(jnp.int32, (C, C), 0))

    outs = [[] for _ in range(nh)]
    for c in range(ts // C):
        sl = slice(c * C, (c + 1) * C)
        for hh in range(nh):
            ka = slice(hh * dk, (hh + 1) * dk)
            vc = v_ref[sl, hh * dv:(hh + 1) * dv]
            a = lax.dot_general(q_dec[sl, ka], k_intra[sl, ka], _NT, preferred_element_type=F32)
            a = jnp.where(causal, a, 0.0).astype(BF)
            st = st_ref[hh]
            o = (jnp.dot(a, vc, preferred_element_type=F32)
                 + lax.dot_general(q_dec[sl, ka], st.astype(BF), _NT, preferred_element_type=F32))
            outs[hh].append(o)
            st_ref[hh] = st * decay[c * C:c * C + 1, ka] + lax.dot_general(
                vc, k_state[sl, ka], _TN, preferred_element_type=F32)

    for hh in range(nh):
        o = jnp.concatenate(outs[hh], axis=0)
        o = o * lax.rsqrt(jnp.mean(o * o, axis=-1, keepdims=True) + NORM_EPS) * ng_ref[...]
        r = r_ref[:, hh * dv:(hh + 1) * dv].astype(F32)
        o_ref[:, hh * dv:(hh + 1) * dv] = (o * (r * _sigmoid(r))).astype(BF)


def _gla_core(P, gz, wgu, bg, ng, *, B, S, ts, nh):
    H, dk, dv = GLA_HEADS, GLA_DK, GLA_DV
    ns = S // ts
    hg = H // nh
    kq = hg
    vv = 2 * H * dk // (nh * dv)
    rr = vv + hg
    return pl.pallas_call(
        functools.partial(_gla_kernel, ts=ts, nh=nh),
        grid=(B, hg, ns),
        in_specs=[
            pl.BlockSpec((ts, nh * dk), lambda b, hh, i: (b * ns + i, hh)),
            pl.BlockSpec((ts, nh * dk), lambda b, hh, i: (b * ns + i, kq + hh)),
            pl.BlockSpec((ts, nh * dv), lambda b, hh, i: (b * ns + i, vv + hh)),
            pl.BlockSpec((ts, nh * dv), lambda b, hh, i: (b * ns + i, rr + hh)),
            pl.BlockSpec((ts, LANE), lambda b, hh, i: (b * ns + i, 0)),
            pl.BlockSpec((LANE, nh * dk), lambda b, hh, i: (0, hh)),
            pl.BlockSpec((1, nh * dk), lambda b, hh, i: (0, hh)),
            pl.BlockSpec((1, dv), lambda b, hh, i: (0, 0)),
        ],
        out_specs=pl.BlockSpec((ts, nh * dv), lambda b, hh, i: (b * ns + i, hh)),
        out_shape=jax.ShapeDtypeStruct((B * S, H * dv), BF),
        scratch_shapes=[pltpu.VMEM((nh, dv, dk), F32)],
        compiler_params=_params("parallel", "parallel", "arbitrary"),
        name="gla_core",
    )(P, P, P, P, gz, wgu, bg.reshape(1, H * dk), ng.reshape(1, dv))


def _gla_layer(h, gain, w_in, w_gate_up, b_gate, norm_g, w_out, *, B, S):
    H, dk, dv = GLA_HEADS, GLA_DK, GLA_DV
    main = 2 * H * dk + 2 * H * dv
    w_main = w_in[:, :main].astype(BF)
    w_gz = jnp.pad(w_in[:, main:], ((0, 0), (0, LANE - GLA_GATE_RANK))).astype(BF)
    P, gz = _proj(h, gain, w_main, w_gz, tm=TM_PROJ, tn=TN_PROJ, tna=LANE)
    wgu = jnp.pad(w_gate_up, ((0, LANE - GLA_GATE_RANK), (0, 0))).astype(BF)
    o = _gla_core(P, gz, wgu, b_gate, norm_g, B=B, S=S, ts=TS_SEQ, nh=GLA_HEADS)
    return _outproj(o, w_out.astype(BF), h, tm=TM)


def kernel(x, p, norm_mix, norm_ffn, norm_ple, norm_final, ffn_w1, ffn_w2, ple_w_proj, ple_w_gate,
           nsa_w_in, nsa_w_out, nsa_cmp_pos_k, nsa_cmp_w1_k, nsa_cmp_w2_k, nsa_cmp_pos_v,
           nsa_cmp_w1_v, nsa_cmp_w2_v, conv_w_in, conv_dw, conv_db, conv_ln_g, conv_ln_b,
           conv_w_out, gla_w_in, gla_w_gate_up, gla_b_gate, gla_norm_g, gla_w_out):
    B, S, D = x.shape
    depth = p.shape[0]
    M = B * S
    h = x.reshape(M, D)
    p2 = p.reshape(depth, M, p.shape[-1])
    ffn_w1, ffn_w2 = ffn_w1.astype(BF), ffn_w2.astype(BF)
    ple_w_gate, ple_w_proj = ple_w_gate.astype(BF), ple_w_proj.astype(BF)
    for i in range(depth):
        m, j = i % 3, i // 3
        if m == 0:
            h = _nsa_layer(h, norm_mix[i], nsa_w_in[j], nsa_w_out[j], nsa_cmp_pos_k[j],
                           nsa_cmp_w1_k[j], nsa_cmp_w2_k[j], nsa_cmp_pos_v[j], nsa_cmp_w1_v[j],
                           nsa_cmp_w2_v[j], B=B, S=S)
        elif m == 1:
            h = _conv_layer(h, norm_mix[i], conv_w_in[j], conv_dw[j], conv_db[j], conv_ln_g[j],
                            conv_ln_b[j], conv_w_out[j], B=B, S=S)
        else:
            h = _gla_layer(h, norm_mix[i], gla_w_in[j], gla_w_gate_up[j], gla_b_gate[j],
                           gla_norm_g[j], gla_w_out[j], B=B, S=S)
        h = _ffn(h, norm_ffn[i], ffn_w1, ffn_w2, i, tm=TM_FFN, tf=TF_FFN)
        h = _ple(h, p2, norm_ple[i], ple_w_gate, ple_w_proj, i, norm_final,
                 final=(i == depth - 1), tm=TM_PLE)
    return h.reshape(B, S, D)
```

```python
import functools

import numpy as np
import jax
import jax.numpy as jnp
from jax import lax
from jax.experimental import pallas as pl
from jax.experimental.pallas import tpu as pltpu

BF = jnp.bfloat16
F32 = jnp.float32

NORM_EPS = 1e-6
LN_EPS = 1e-5
NEG = -1e30
LANE = 128
VMEM_LIMIT = 56 * 1024 * 1024

NSA_HEADS = 16
NSA_KV_HEADS = 4
NSA_HPG = NSA_HEADS // NSA_KV_HEADS
NSA_DH = 128
NSA_CMP_LEN = 32
NSA_CMP_STRIDE = 16
NSA_SEL_LEN = 64
NSA_SEL_TOPK = 16
NSA_SEL_LOCAL = 2
NSA_WINDOW = 512
NSA_FORCE = 1e9

SUM_ROWS = 16

CONV_WIDTH = 31
CONV_HALO = 32

GLA_HEADS = 4
GLA_DK = 256
GLA_DV = 512
GLA_GATE_RANK = 16
GLA_GATE_NORM = 16.0
GLA_CHUNK = 64

TM = 512
TM_FFN = 1024
TM_PLE = 256
TN_PROJ = 2048
TN_GLU = 1024
TF_FFN = 512
TQ_NSA = 512
TK_NSA = 512
TS_SEQ = 256

_LOG2E = 1.4426950408889634
_SEL_SHIFT = NSA_SEL_LEN.bit_length() - 1
_CHUNK_SHIFT = GLA_CHUNK.bit_length() - 1

_NT = (((1,), (1,)), ((), ()))
_TN = (((0,), (0,)), ((), ()))


def _params(*sem):
    return pltpu.CompilerParams(dimension_semantics=sem, vmem_limit_bytes=VMEM_LIMIT)


def _sigmoid(x):
    return 1.0 / (1.0 + jnp.exp(-x))


def _rms_bf16(x, g):
    ms = jnp.mean(x * x, axis=-1, keepdims=True)
    return (x * lax.rsqrt(ms + NORM_EPS) * g).astype(BF)


def _proj_kernel(x_ref, g_ref, w_ref, wa_ref, o_ref, oa_ref, xn_ref):
    @pl.when(pl.program_id(1) == 0)
    def _():
        xn = _rms_bf16(x_ref[...], g_ref[...])
        xn_ref[...] = xn
        oa_ref[...] = jnp.dot(xn, wa_ref[...], preferred_element_type=F32)

    o_ref[...] = jnp.dot(xn_ref[...], w_ref[...], preferred_element_type=F32).astype(o_ref.dtype)


def _proj(x, gain, w, w_aux, *, tm, tn):
    M, K = x.shape
    N = w.shape[1]
    Na = w_aux.shape[1]
    return pl.pallas_call(
        _proj_kernel,
        grid=(M // tm, N // tn),
        in_specs=[
            pl.BlockSpec((tm, K), lambda i, j: (i, 0)),
            pl.BlockSpec((1, K), lambda i, j: (0, 0)),
            pl.BlockSpec((K, tn), lambda i, j: (0, j)),
            pl.BlockSpec((K, Na), lambda i, j: (0, 0), pipeline_mode=pl.Buffered(1)),
        ],
        out_specs=[
            pl.BlockSpec((tm, tn), lambda i, j: (i, j)),
            pl.BlockSpec((tm, Na), lambda i, j: (i, 0)),
        ],
        out_shape=[jax.ShapeDtypeStruct((M, N), BF), jax.ShapeDtypeStruct((M, Na), F32)],
        scratch_shapes=[pltpu.VMEM((tm, K), BF)],
        compiler_params=_params("parallel", "arbitrary"),
        name="proj",
    )(x, gain.reshape(1, K), w, w_aux)


def _glu_kernel(x_ref, g_ref, wa_ref, wg_ref, o_ref, xn_ref):
    @pl.when(pl.program_id(1) == 0)
    def _():
        xn_ref[...] = _rms_bf16(x_ref[...], g_ref[...])

    xn = xn_ref[...]
    a = jnp.dot(xn, wa_ref[...], preferred_element_type=F32)
    g = jnp.dot(xn, wg_ref[...], preferred_element_type=F32)
    o_ref[...] = a * _sigmoid(g)


def _glu_proj(x, gain, w, *, tm, tn):
    M, K = x.shape
    C = w.shape[1] // 2
    nj = C // tn
    return pl.pallas_call(
        _glu_kernel,
        grid=(M // tm, nj),
        in_specs=[
            pl.BlockSpec((tm, K), lambda i, j: (i, 0)),
            pl.BlockSpec((1, K), lambda i, j: (0, 0)),
            pl.BlockSpec((K, tn), lambda i, j: (0, j)),
            pl.BlockSpec((K, tn), lambda i, j: (0, j + nj)),
        ],
        out_specs=pl.BlockSpec((tm, tn), lambda i, j: (i, j)),
        out_shape=jax.ShapeDtypeStruct((M, C), F32),
        scratch_shapes=[pltpu.VMEM((tm, K), BF)],
        compiler_params=_params("parallel", "arbitrary"),
        name="glu_proj",
    )(x, gain.reshape(1, K), w, w)


def _outproj_kernel(a_ref, w_ref, h_ref, o_ref):
    o_ref[...] = h_ref[...] + jnp.dot(a_ref[...], w_ref[...], preferred_element_type=F32)


def _outproj(a, w, h, *, tm):
    M, K = a.shape
    N = w.shape[1]
    return pl.pallas_call(
        _outproj_kernel,
        grid=(M // tm,),
        in_specs=[
            pl.BlockSpec((tm, K), lambda i: (i, 0)),
            pl.BlockSpec((K, N), lambda i: (0, 0)),
            pl.BlockSpec((tm, N), lambda i: (i, 0)),
        ],
        out_specs=pl.BlockSpec((tm, N), lambda i: (i, 0)),
        out_shape=jax.ShapeDtypeStruct((M, N), F32),
        compiler_params=_params("parallel"),
        name="outproj",
    )(a, w, h)


def _ffn_kernel(x_ref, g_ref, w1_ref, w2_ref, o_ref, xn_ref):
    @pl.when(pl.program_id(1) == 0)
    def _():
        x = x_ref[...]
        xn_ref[...] = _rms_bf16(x, g_ref[...])
        o_ref[...] = x

    a = jnp.dot(xn_ref[...], w1_ref[...], preferred_element_type=F32)
    a = jnp.square(jnp.maximum(a, 0.0)).astype(BF)
    o_ref[...] += jnp.dot(a, w2_ref[...], preferred_element_type=F32)


def _ffn(x, gain, w1, w2, layer, *, tm, tf):
    M, K = x.shape
    Fd = w1.shape[2]
    return pl.pallas_call(
        _ffn_kernel,
        grid=(M // tm, Fd // tf),
        in_specs=[
            pl.BlockSpec((tm, K), lambda i, j: (i, 0)),
            pl.BlockSpec((1, K), lambda i, j: (0, 0)),
            pl.BlockSpec((None, K, tf), lambda i, j: (layer, 0, j)),
            pl.BlockSpec((None, tf, K), lambda i, j: (layer, j, 0)),
        ],
        out_specs=pl.BlockSpec((tm, K), lambda i, j: (i, 0)),
        out_shape=jax.ShapeDtypeStruct((M, K), F32),
        scratch_shapes=[pltpu.VMEM((tm, K), BF)],
        compiler_params=_params("parallel", "arbitrary"),
        name="ffn",
    )(x, gain.reshape(1, K), w1, w2)


def _ple_kernel(h_ref, p_ref, g_ref, wg_ref, wp_ref, gf_ref, o_ref, *, final):
    h = h_ref[...]
    gate = _sigmoid(jnp.dot(_rms_bf16(h, g_ref[...]), wg_ref[...], preferred_element_type=F32))
    proj = jnp.dot(p_ref[...].astype(BF), wp_ref[...], preferred_element_type=F32)
    out = h + proj * gate
    if final:
        ms = jnp.mean(out * out, axis=-1, keepdims=True)
        out = out * lax.rsqrt(ms + NORM_EPS) * gf_ref[...]
    o_ref[...] = out


def _ple(h, p, gain, w_gate, w_proj, layer, gain_final, *, final, tm):
    M, K = h.shape
    Pd = p.shape[2]
    return pl.pallas_call(
        functools.partial(_ple_kernel, final=final),
        grid=(M // tm,),
        in_specs=[
            pl.BlockSpec((tm, K), lambda i: (i, 0)),
            pl.BlockSpec((None, tm, Pd), lambda i: (layer, i, 0)),
            pl.BlockSpec((1, K), lambda i: (0, 0)),
            pl.BlockSpec((None, K, K), lambda i: (layer, 0, 0)),
            pl.BlockSpec((None, Pd, K), lambda i: (layer, 0, 0)),
            pl.BlockSpec((1, K), lambda i: (0, 0)),
        ],
        out_specs=pl.BlockSpec((tm, K), lambda i: (i, 0)),
        out_shape=jax.ShapeDtypeStruct((M, K), F32),
        compiler_params=_params("parallel"),
        name="ple",
    )(h, p, gain.reshape(1, K), w_gate, w_proj, gain_final.reshape(1, K))


def _compress_kernel(x_ref, pos_ref, w1_ref, w2_ref, o_ref):
    Dh = x_ref.shape[-1]
    R = x_ref.shape[0] // NSA_CMP_STRIDE
    nl = NSA_CMP_STRIDE
    a = jnp.zeros((R, w1_ref.shape[-1]), F32)
    b = jnp.zeros((R, w1_ref.shape[-1]), F32)
    for l in range(nl):
        xl = x_ref[pl.ds(l, R, stride=nl), :]
        a = a + jnp.dot((xl + pos_ref[0, l:l + 1, :]).astype(BF),
                        w1_ref[0, l * Dh:(l + 1) * Dh, :], preferred_element_type=F32)
        b = b + jnp.dot((xl + pos_ref[0, nl + l:nl + l + 1, :]).astype(BF),
                        w1_ref[0, (nl + l) * Dh:(nl + l + 1) * Dh, :], preferred_element_type=F32)
    hid = a + pltpu.roll(b, R - 1, 0)
    hid = hid * _sigmoid(hid)
    out = jnp.dot(hid.astype(BF), w2_ref[0], preferred_element_type=F32)
    row = lax.broadcasted_iota(jnp.int32, out.shape, 0)
    o_ref[0, 0, 0] = jnp.where(row < R - 1, out, 0.0).astype(BF)


def _compress(aux, pos, w1, w2, *, B, S, col0):
    G, Dh = NSA_KV_HEADS, NSA_DH
    R = S // NSA_CMP_STRIDE
    return pl.pallas_call(
        _compress_kernel,
        grid=(B, 2, G),
        in_specs=[
            pl.BlockSpec((S, Dh), lambda b, c, g: (b, col0 + c * G + g)),
            pl.BlockSpec((1,) + pos.shape[1:], lambda b, c, g: (c, 0, 0)),
            pl.BlockSpec((1,) + w1.shape[1:], lambda b, c, g: (c, 0, 0)),
            pl.BlockSpec((1,) + w2.shape[1:], lambda b, c, g: (c, 0, 0)),
        ],
        out_specs=pl.BlockSpec((1, 1, 1, R, Dh), lambda b, c, g: (b, c, g, 0, 0)),
        out_shape=jax.ShapeDtypeStruct((B, 2, G, R, Dh), BF),
        compiler_params=_params("parallel", "parallel", "parallel"),
        name="nsa_compress",
    )(aux, pos, w1, w2)


def _nsa_kernel(q_ref, gt_ref, kc_ref, vc_ref, ks_ref, vs_ref, kw_ref, vw_ref, ovt_ref, o_ref,
                ksa_ref, vst_ref, vwt_ref, m_ref, acc_ref, sa_ref, sb_ref, ma_ref, mb_ref, ow_ref, *, tq, tk, n_top):
    Dh = NSA_DH
    i = pl.program_id(2)
    t0 = i * tq
    S = ks_ref.shape[0]
    tc = 512

    @pl.when(i == 0)
    def _():
        lane = lax.broadcasted_iota(jnp.int32, (tc, LANE), 1)
        krow = lax.broadcasted_iota(jnp.int32, (tc, LANE), 0)
        ones_rows = jnp.where(lax.broadcasted_iota(jnp.int32, (SUM_ROWS, tc), 0) == 0, 1.0, 0.0).astype(BF)
        for c in range(S // tc):
            sl = slice(c * tc, (c + 1) * tc)
            vst_ref[0:Dh, sl] = vs_ref[sl, :].astype(F32).T.astype(BF)
            vst_ref[Dh:Dh + SUM_ROWS, sl] = ones_rows
            vwt_ref[0:Dh, sl] = vw_ref[sl, :].astype(F32).T.astype(BF)
            vwt_ref[Dh:Dh + SUM_ROWS, sl] = ones_rows
            ksa_ref[sl, 0:Dh] = ks_ref[sl, :]
            ksa_ref[sl, Dh:Dh + LANE] = jnp.where(
                jnp.right_shift(c * tc + krow, _SEL_SHIFT) == lane, 1.0, 0.0).astype(BF)

    H = NSA_HPG
    tcol = t0 + lax.broadcasted_iota(jnp.int32, (1, tq), 1)
    qst = jnp.concatenate([q_ref[:, h * Dh:(h + 1) * Dh] for h in range(H)], axis=0)
    gtT = _sigmoid(gt_ref[...]).T

    def heads(x):
        return jnp.concatenate([x] * H, axis=1)

    kc = kc_ref[0, 0, 0]
    vcT = vc_ref[0, 0, 0].astype(F32).T.astype(BF)
    ncp = kc.shape[0]
    n_row = lax.broadcasted_iota(jnp.int32, (ncp, tq), 0)
    valid_c = heads((n_row * NSA_CMP_STRIDE + (NSA_CMP_LEN - 1)) <= tcol)
    s = lax.dot_general(kc, qst, _NT, preferred_element_type=F32)
    s = jnp.where(valid_c, s, NEG)
    e = jnp.exp2(s - jnp.max(s, axis=0, keepdims=True))
    den = jnp.sum(e, axis=0, keepdims=True)
    p = jnp.where(valid_c, e, 0.0) * (1.0 / den)
    o_c = jnp.dot(vcT, p.astype(BF), preferred_element_type=F32)
    psum = p[:, 0:tq]
    for h in range(1, H):
        psum = psum + p[:, h * tq:(h + 1) * tq]

    hw = (H // 2) * tq

    wlen = NSA_WINDOW + tq
    ws = pl.multiple_of(jnp.maximum(t0 - NSA_WINDOW, 0), tq)
    kwt = kw_ref[pl.ds(ws, wlen), :]
    vwT = vwt_ref[:, pl.ds(ws, wlen)]
    kpos_w = ws + lax.broadcasted_iota(jnp.int32, (wlen, tq), 0)
    ok_w = (kpos_w <= tcol) & (kpos_w > tcol - NSA_WINDOW)
    bias_w = jnp.where(ok_w, 0.0, NEG)
    bias_w2 = jnp.concatenate([bias_w] * (H // 2), axis=1)
    sw = [lax.dot_general(kwt, qst[c * hw:(c + 1) * hw], _NT, preferred_element_type=F32) + bias_w2
          for c in range(2)]
    o_w = []
    for c in range(2):
        e = jnp.exp2(sw[c] - jnp.max(sw[c], axis=0, keepdims=True)).astype(BF)
        ow = jnp.dot(vwT, e, preferred_element_type=F32)
        o_w.append(ow[0:Dh] * (1.0 / ow[Dh:Dh + 1]))
    ow_ref[...] = jnp.concatenate(o_w, axis=1)

    ovt = ovt_ref[...]
    p_hi = psum.astype(BF)
    p_lo = (psum - p_hi.astype(F32)).astype(BF)
    imp = (jnp.dot(ovt, p_hi, preferred_element_type=F32)
           + jnp.dot(ovt, p_lo, preferred_element_type=F32))
    n_sel = imp.shape[0]
    m_idx = lax.broadcasted_iota(jnp.int32, imp.shape, 0)
    m_idx_f = m_idx.astype(F32)
    diff = jnp.right_shift(tcol, _SEL_SHIFT) - m_idx
    forced = (m_idx == 0) | ((diff >= 0) & (diff < NSA_SEL_LOCAL))
    member = jnp.where(forced, 1.0, 0.0)
    work = jnp.where(forced, -3e38, jnp.where(diff >= 0, imp, -NSA_FORCE))
    for _ in range(n_top - 1 - NSA_SEL_LOCAL):
        mx = jnp.max(work, axis=0, keepdims=True)
        first = jnp.min(jnp.where(work == mx, m_idx_f, 1e6), axis=0, keepdims=True)
        pick = m_idx_f == first
        member = jnp.where(pick, 1.0, member)
        work = jnp.where(pick, -3e38, work)
    sel_bias = jnp.where((member > 0.5) & (diff >= 0), 0.0, NEG)
    sel_bias = jnp.concatenate([sel_bias, jnp.zeros((LANE - n_sel, tq), F32)], axis=0)
    sb = sel_bias.T.astype(BF)
    q_aug = jnp.concatenate([qst, jnp.concatenate([sb] * H, axis=0)], axis=1)

    m_ref[...] = jnp.full(m_ref.shape, NEG, F32)
    acc_ref[...] = jnp.zeros(acc_ref.shape, F32)
    key_row = lax.broadcasted_iota(jnp.int32, (tk, tq), 0)

    def produce(j, s_ref, mt_ref, diagonal):
        k0 = pl.multiple_of(j * tk, tk)
        s = lax.dot_general(ksa_ref[pl.ds(k0, tk), :], q_aug, _NT, preferred_element_type=F32)
        if diagonal:
            s = jnp.where(heads((k0 + key_row) <= tcol), s, NEG)
        s_ref[...] = s
        mt_ref[...] = jnp.max(s, axis=0, keepdims=True)

    def consume(j, s_ref, mt_ref):
        k0 = pl.multiple_of(j * tk, tk)
        va = vst_ref[:, pl.ds(k0, tk)]
        for c in range(2):
            sl = slice(c * hw, (c + 1) * hw)
            m_prev = m_ref[:, sl]
            m_new = jnp.maximum(m_prev, mt_ref[:, sl])
            alpha = jnp.exp2(m_prev - m_new)
            e = jnp.exp2(s_ref[:, sl] - m_new).astype(BF)
            acc_ref[:, sl] = alpha * acc_ref[:, sl] + jnp.dot(va, e, preferred_element_type=F32)
            m_ref[:, sl] = m_new

    n_full = (t0 + tq + tk - 1) // tk - 1
    start = n_full % 2

    @pl.when(n_full == 0)
    def _():
        produce(0, sa_ref, ma_ref, True)

    @pl.when((n_full > 0) & (start == 0))
    def _():
        produce(0, sa_ref, ma_ref, False)

    @pl.when(n_full == 1)
    def _():
        produce(0, sb_ref, mb_ref, False)
        produce(1, sa_ref, ma_ref, True)
        consume(0, sb_ref, mb_ref)

    @pl.when((n_full > 1) & (start == 1))
    def _():
        produce(0, sb_ref, mb_ref, False)
        produce(1, sa_ref, ma_ref, False)
        consume(0, sb_ref, mb_ref)

    def sel_pair(jj, carry):
        ta = start + 2 * jj
        produce(ta + 1, sb_ref, mb_ref, False)
        consume(ta, sa_ref, ma_ref)
        produce(ta + 2, sa_ref, ma_ref, False)
        consume(ta + 1, sb_ref, mb_ref)
        return carry

    n_pairs = n_full // 2
    lax.fori_loop(0, n_pairs - 1, sel_pair, 0)

    @pl.when(n_pairs > 0)
    def _():
        ta = n_full - 2
        produce(ta + 1, sb_ref, mb_ref, False)
        consume(ta, sa_ref, ma_ref)
        produce(ta + 2, sa_ref, ma_ref, True)
        consume(ta + 1, sb_ref, mb_ref)

    consume(n_full, sa_ref, ma_ref)

    o_w = ow_ref[...]
    o_s = acc_ref[0:Dh, :] * (1.0 / acc_ref[Dh:Dh + 1, :])

    for h in range(H):
        sl = slice(h * tq, (h + 1) * tq)
        out = (gtT[3 * h:3 * h + 1, :] * o_c[:, sl] + gtT[3 * h + 1:3 * h + 2, :] * o_s[:, sl]
               + gtT[3 * h + 2:3 * h + 3, :] * o_w[:, sl])
        o_ref[:, h * Dh:(h + 1) * Dh] = out.T.astype(BF)


def _nsa_core(P, gates, cmp_kv, ovt, *, B, S, tq, tk):
    G, HPG, Dh = NSA_KV_HEADS, NSA_HPG, NSA_DH
    nq = S // tq
    ncp = cmp_kv.shape[3]
    n_sel = S // NSA_SEL_LEN
    n_top = min(NSA_SEL_TOPK, n_sel)
    assert S >= NSA_WINDOW + tq and NSA_WINDOW % tq == 0 and S % tk == 0 and tk % NSA_SEL_LEN == 0
    assert S % 512 == 0
    qcols = NSA_HEADS * Dh // LANE
    kvcols = G

    def kv_spec(slab):
        off = qcols + slab * kvcols
        return pl.BlockSpec((S, Dh), lambda b, g, i: (b, off + g))

    return pl.pallas_call(
        functools.partial(_nsa_kernel, tq=tq, tk=tk, n_top=n_top),
        grid=(B, G, nq),
        in_specs=[
            pl.BlockSpec((tq, HPG * Dh), lambda b, g, i: (b * nq + i, g)),
            pl.BlockSpec((tq, LANE), lambda b, g, i: (b * nq + i, g)),
            pl.BlockSpec((1, 1, 1, ncp, Dh), lambda b, g, i: (b, 0, g, 0, 0)),
            pl.BlockSpec((1, 1, 1, ncp, Dh), lambda b, g, i: (b, 1, g, 0, 0)),
            kv_spec(0), kv_spec(1), kv_spec(2), kv_spec(3),
            pl.BlockSpec(ovt.shape, lambda b, g, i: (0, 0)),
        ],
        out_specs=pl.BlockSpec((tq, HPG * Dh), lambda b, g, i: (b * nq + i, g)),
        out_shape=jax.ShapeDtypeStruct((B * S, NSA_HEADS * Dh), BF),
        scratch_shapes=[
            pltpu.VMEM((S, Dh + LANE), BF),
            pltpu.VMEM((Dh + SUM_ROWS, S), BF),
            pltpu.VMEM((Dh + SUM_ROWS, S), BF),
            pltpu.VMEM((1, HPG * tq), F32),
            pltpu.VMEM((Dh + SUM_ROWS, HPG * tq), F32),
            pltpu.VMEM((tk, HPG * tq), F32),
            pltpu.VMEM((tk, HPG * tq), F32),
            pltpu.VMEM((1, HPG * tq), F32),
            pltpu.VMEM((1, HPG * tq), F32),
            pltpu.VMEM((Dh, HPG * tq), F32),
        ],
        compiler_params=_params("parallel", "parallel", "arbitrary"),
        name="nsa_core",
    )(P, gates, cmp_kv, cmp_kv, P, P, P, P, ovt)


def _overlap_matrix_t(S, ncp):
    n_cmp = (S - NSA_CMP_LEN) // NSA_CMP_STRIDE + 1
    n_sel = S // NSA_SEL_LEN
    cs = np.arange(n_cmp) * NSA_CMP_STRIDE
    ss = np.arange(n_sel) * NSA_SEL_LEN
    ovl = ((cs[:, None] <= ss[None, :] + NSA_SEL_LEN - 1)
           & (cs[:, None] + NSA_CMP_LEN - 1 >= ss[None, :])).astype(np.float32)
    out = np.zeros((n_sel, ncp), np.float32)
    out[:, :n_cmp] = ovl.T
    return jnp.asarray(out, BF)


def _nsa_layer(h, gain, w_in, w_out, pos_k, w1_k, w2_k, pos_v, w1_v, w2_v, *, B, S):
    G, HPG, Dh = NSA_KV_HEADS, NSA_HPG, NSA_DH
    D = h.shape[1]
    qw, kvw = NSA_HEADS * Dh, G * Dh
    main = qw + 6 * kvw
    w_main = jnp.concatenate([w_in[:, :qw] * (Dh ** -0.5 * _LOG2E), w_in[:, qw + 2 * kvw:main]],
                             axis=1).astype(BF)
    wg = w_in[:, main:].reshape(D, G, 3 * HPG)
    wg = jnp.pad(wg, ((0, 0), (0, 0), (0, LANE - 3 * HPG))).reshape(D, G * LANE)
    w_aux = jnp.concatenate([wg, w_in[:, qw:qw + 2 * kvw]], axis=1).astype(BF)
    P, aux = _proj(h, gain, w_main, w_aux, tm=TM, tn=TN_PROJ)

    pos = jnp.stack([pos_k, pos_v])
    w1 = jnp.stack([w1_k, w1_v]).astype(BF)
    w2 = jnp.stack([w2_k, w2_v]).astype(BF)
    cmp_kv = _compress(aux, pos, w1, w2, B=B, S=S, col0=G)

    ovt = _overlap_matrix_t(S, S // NSA_CMP_STRIDE)
    o = _nsa_core(P, aux, cmp_kv, ovt, B=B, S=S, tq=TQ_NSA, tk=TK_NSA)
    return _outproj(o, w_out.astype(BF), h, tm=TM)


def _conv_kernel(u_ref, halo_ref, dw_ref, db_ref, lg_ref, lb_ref, w_ref, h_ref, o_ref,
                 buf_ref, sh_ref, y_ref, *, ts):
    C = u_ref.shape[-1]
    halo = halo_ref[...]
    buf_ref[0:CONV_HALO, :] = jnp.where(pl.program_id(1) > 0, halo, 0.0)
    buf_ref[CONV_HALO:CONV_HALO + ts, :] = u_ref[...]
    off = CONV_HALO - (CONV_WIDTH - 1)
    rs = 128
    sub = 8
    span = ts + CONV_HALO - sub

    def chan_body(c, carry):
        c0 = pl.multiple_of(c * LANE, LANE)
        for r in range(1, sub):
            sh_ref[r, 0:span, :] = buf_ref[pl.ds(r, span), pl.ds(c0, LANE)]
        for rr in range(ts // rs):
            acc = jnp.zeros((rs, LANE), F32) + db_ref[:, pl.ds(c0, LANE)]
            for k in range(CONV_WIDTH):
                r = (off + k) % sub
                base = rr * rs + off + k - r
                if r == 0:
                    win = buf_ref[pl.ds(base, rs), pl.ds(c0, LANE)]
                else:
                    win = sh_ref[r, pl.ds(base, rs), :]
                acc = acc + win * dw_ref[k:k + 1, pl.ds(c0, LANE)]
            y_ref[pl.ds(rr * rs, rs), pl.ds(c0, LANE)] = acc
        return carry

    lax.fori_loop(0, C // LANE, chan_body, 0)

    y = y_ref[...]
    mu = jnp.mean(y, axis=-1, keepdims=True)
    yc = y - mu
    var = jnp.mean(yc * yc, axis=-1, keepdims=True)
    z = yc * lax.rsqrt(var + LN_EPS) * lg_ref[...] + lb_ref[...]
    z = (z * _sigmoid(z)).astype(BF)
    o_ref[...] = h_ref[...] + jnp.dot(z, w_ref[...], preferred_element_type=F32)


def _conv_tail(u, dw, db, ln_g, ln_b, w_out, h, *, B, S, ts):
    M, C = u.shape
    D = w_out.shape[1]
    ns = S // ts
    hb = ts // CONV_HALO
    dwp = jnp.pad(dw, ((0, CONV_HALO - CONV_WIDTH), (0, 0)))
    return pl.pallas_call(
        functools.partial(_conv_kernel, ts=ts),
        grid=(B, ns),
        in_specs=[
            pl.BlockSpec((ts, C), lambda b, i: (b * ns + i, 0)),
            pl.BlockSpec((CONV_HALO, C), lambda b, i: (jnp.maximum((b * ns + i) * hb - 1, 0), 0)),
            pl.BlockSpec((CONV_HALO, C), lambda b, i: (0, 0)),
            pl.BlockSpec((1, C), lambda b, i: (0, 0)),
            pl.BlockSpec((1, C), lambda b, i: (0, 0)),
            pl.BlockSpec((1, C), lambda b, i: (0, 0)),
            pl.BlockSpec((C, D), lambda b, i: (0, 0)),
            pl.BlockSpec((ts, D), lambda b, i: (b * ns + i, 0)),
        ],
        out_specs=pl.BlockSpec((ts, D), lambda b, i: (b * ns + i, 0)),
        out_shape=jax.ShapeDtypeStruct((M, D), F32),
        scratch_shapes=[pltpu.VMEM((CONV_HALO + ts, C), F32), pltpu.VMEM((8, CONV_HALO + ts, LANE), F32),
                        pltpu.VMEM((ts, C), F32)],
        compiler_params=_params("parallel", "parallel"),
        name="conv_tail",
    )(u, u, dwp, db.reshape(1, C), ln_g.reshape(1, C), ln_b.reshape(1, C), w_out, h)


def _conv_layer(h, gain, w_in, dw, db, ln_g, ln_b, w_out, *, B, S):
    u = _glu_proj(h, gain, w_in.astype(BF), tm=TM, tn=TN_GLU)
    return _conv_tail(u, dw, db, ln_g, ln_b, w_out.astype(BF), h, B=B, S=S, ts=TS_SEQ)


def _gla_kernel(q_ref, k_ref, v_ref, r_ref, gz_ref, wgu_ref, bg_ref, ng_ref, o_ref, st_ref, *, ts, nh):
    C = GLA_CHUNK
    dk, dv = GLA_DK, GLA_DV

    @pl.when(pl.program_id(2) == 0)
    def _():
        st_ref[...] = jnp.zeros(st_ref.shape, F32)

    z = jnp.dot(gz_ref[...].astype(BF), wgu_ref[...], preferred_element_type=F32) + bg_ref[...]
    glog = (jnp.minimum(z, 0.0) - jnp.log(1.0 + jnp.exp(-jnp.abs(z)))) / GLA_GATE_NORM

    g1 = glog.astype(BF)
    r1 = glog - g1.astype(F32)
    g2 = r1.astype(BF)
    g3 = (r1 - g2.astype(F32)).astype(BF)
    row = lax.broadcasted_iota(jnp.int32, (ts, ts), 0)
    col = lax.broadcasted_iota(jnp.int32, (ts, ts), 1)
    same = jnp.right_shift(row, _CHUNK_SHIFT) == jnp.right_shift(col, _CHUNK_SHIFT)
    tri = jnp.where(same & (col <= row), 1.0, 0.0).astype(BF)
    full = jnp.where(same, 1.0, 0.0).astype(BF)

    def msum(mat):
        return (jnp.dot(mat, g1, preferred_element_type=F32)
                + jnp.dot(mat, g2, preferred_element_type=F32)
                + jnp.dot(mat, g3, preferred_element_type=F32))

    bcum = msum(tri)
    blast = msum(full)

    q = q_ref[...].astype(F32) * (dk ** -0.5)
    k = k_ref[...].astype(F32)
    q_dec = (q * jnp.exp(bcum)).astype(BF)
    k_intra = (k * jnp.exp(-bcum)).astype(BF)
    k_state = (k * jnp.exp(blast - bcum)).astype(BF)
    decay = jnp.exp(blast)
    causal = (lax.broadcasted_iota(jnp.int32, (C, C), 1) <= lax.broadcasted_iota(jnp.int32, (C, C), 0))

    outs = [[] for _ in range(nh)]
    for c in range(ts // C):
        sl = slice(c * C, (c + 1) * C)
        for hh in range(nh):
            ka = slice(hh * dk, (hh + 1) * dk)
            vc = v_ref[sl, hh * dv:(hh + 1) * dv]
            a = lax.dot_general(q_dec[sl, ka], k_intra[sl, ka], _NT, preferred_element_type=F32)
            a = jnp.where(causal, a, 0.0).astype(BF)
            st = st_ref[hh]
            o = (jnp.dot(a, vc, preferred_element_type=F32)
                 + lax.dot_general(q_dec[sl, ka], st.astype(BF), _NT, preferred_element_type=F32))
            outs[hh].append(o)
            st_ref[hh] = st * decay[c * C:c * C + 1, ka] + lax.dot_general(
                vc, k_state[sl, ka], _TN, preferred_element_type=F32)

    for hh in range(nh):
        o = jnp.concatenate(outs[hh], axis=0)
        o = o * lax.rsqrt(jnp.mean(o * o, axis=-1, keepdims=True) + NORM_EPS) * ng_ref[...]
        r = r_ref[:, hh * dv:(hh + 1) * dv].astype(F32)
        o_ref[:, hh * dv:(hh + 1) * dv] = (o * (r * _sigmoid(r))).astype(BF)


def _gla_core(P, gz, wgu, bg, ng, *, B, S, ts, nh):
    H, dk, dv = GLA_HEADS, GLA_DK, GLA_DV
    ns = S // ts
    hg = H // nh
    kq = hg
    vv = 2 * H * dk // (nh * dv)
    rr = vv + hg
    return pl.pallas_call(
        functools.partial(_gla_kernel, ts=ts, nh=nh),
        grid=(B, hg, ns),
        in_specs=[
            pl.BlockSpec((ts, nh * dk), lambda b, hh, i: (b * ns + i, hh)),
            pl.BlockSpec((ts, nh * dk), lambda b, hh, i: (b * ns + i, kq + hh)),
            pl.BlockSpec((ts, nh * dv), lambda b, hh, i: (b * ns + i, vv + hh)),
            pl.BlockSpec((ts, nh * dv), lambda b, hh, i: (b * ns + i, rr + hh)),
            pl.BlockSpec((ts, LANE), lambda b, hh, i: (b * ns + i, 0)),
            pl.BlockSpec((LANE, nh * dk), lambda b, hh, i: (0, hh)),
            pl.BlockSpec((1, nh * dk), lambda b, hh, i: (0, hh)),
            pl.BlockSpec((1, dv), lambda b, hh, i: (0, 0)),
        ],
        out_specs=pl.BlockSpec((ts, nh * dv), lambda b, hh, i: (b * ns + i, hh)),
        out_shape=jax.ShapeDtypeStruct((B * S, H * dv), BF),
        scratch_shapes=[pltpu.VMEM((nh, dv, dk), F32)],
        compiler_params=_params("parallel", "parallel", "arbitrary"),
        name="gla_core",
    )(P, P, P, P, gz, wgu, bg.reshape(1, H * dk), ng.reshape(1, dv))


def _gla_layer(h, gain, w_in, w_gate_up, b_gate, norm_g, w_out, *, B, S):
    H, dk, dv = GLA_HEADS, GLA_DK, GLA_DV
    main = 2 * H * dk + 2 * H * dv
    w_main = w_in[:, :main].astype(BF)
    w_gz = jnp.pad(w_in[:, main:], ((0, 0), (0, LANE - GLA_GATE_RANK))).astype(BF)
    P, gz = _proj(h, gain, w_main, w_gz, tm=TM, tn=TN_PROJ)
    wgu = jnp.pad(w_gate_up, ((0, LANE - GLA_GATE_RANK), (0, 0))).astype(BF)
    o = _gla_core(P, gz, wgu, b_gate, norm_g, B=B, S=S, ts=TS_SEQ, nh=GLA_HEADS)
    return _outproj(o, w_out.astype(BF), h, tm=TM)


def kernel(x, p, norm_mix, norm_ffn, norm_ple, norm_final, ffn_w1, ffn_w2, ple_w_proj, ple_w_gate,
           nsa_w_in, nsa_w_out, nsa_cmp_pos_k, nsa_cmp_w1_k, nsa_cmp_w2_k, nsa_cmp_pos_v,
           nsa_cmp_w1_v, nsa_cmp_w2_v, conv_w_in, conv_dw, conv_db, conv_ln_g, conv_ln_b,
           conv_w_out, gla_w_in, gla_w_gate_up, gla_b_gate, gla_norm_g, gla_w_out):
    B, S, D = x.shape
    depth = p.shape[0]
    M = B * S
    h = x.reshape(M, D)
    p2 = p.reshape(depth, M, p.shape[-1])
    ffn_w1, ffn_w2 = ffn_w1.astype(BF), ffn_w2.astype(BF)
    ple_w_gate, ple_w_proj = ple_w_gate.astype(BF), ple_w_proj.astype(BF)
    for i in range(depth):
        m, j = i % 3, i // 3
        if m == 0:
            h = _nsa_layer(h, norm_mix[i], nsa_w_in[j], nsa_w_out[j], nsa_cmp_pos_k[j],
                           nsa_cmp_w1_k[j], nsa_cmp_w2_k[j], nsa_cmp_pos_v[j], nsa_cmp_w1_v[j],
                           nsa_cmp_w2_v[j], B=B, S=S)
        elif m == 1:
            h = _conv_layer(h, norm_mix[i], conv_w_in[j], conv_dw[j], conv_db[j], conv_ln_g[j],
                            conv_ln_b[j], conv_w_out[j], B=B, S=S)
        else:
            h = _gla_layer(h, norm_mix[i], gla_w_in[j], gla_w_gate_up[j], gla_b_gate[j],
                           gla_norm_g[j], gla_w_out[j], B=B, S=S)
        h = _ffn(h, norm_ffn[i], ffn_w1, ffn_w2, i, tm=TM_FFN, tf=TF_FFN)
        h = _ple(h, p2, norm_ple[i], ple_w_gate, ple_w_proj, i, norm_final,
                 final=(i == depth - 1), tm=TM_PLE)
    return h.reshape(B, S, D)
```

```python
import functools

import numpy as np
import jax
import jax.numpy as jnp
from jax import lax
from jax.experimental import pallas as pl
from jax.experimental.pallas import tpu as pltpu

BF = jnp.bfloat16
F32 = jnp.float32

NORM_EPS = 1e-6
LN_EPS = 1e-5
NEG = -1e30
LANE = 128
VMEM_LIMIT = 56 * 1024 * 1024

NSA_HEADS = 16
NSA_KV_HEADS = 4
NSA_HPG = NSA_HEADS // NSA_KV_HEADS
NSA_DH = 128
NSA_CMP_LEN = 32
NSA_CMP_STRIDE = 16
NSA_SEL_LEN = 64
NSA_SEL_TOPK = 16
NSA_SEL_LOCAL = 2
NSA_WINDOW = 512
NSA_FORCE = 1e9

SUM_ROWS = 16

CONV_WIDTH = 31
CONV_HALO = 32

GLA_HEADS = 4
GLA_DK = 256
GLA_DV = 512
GLA_GATE_RANK = 16
GLA_GATE_NORM = 16.0
GLA_CHUNK = 64

TM = 512
TM_FFN = 1024
TM_PLE = 512
TN_PROJ = 2048
TN_GLU = 1024
TF_FFN = 512
TQ_NSA = 512
TK_NSA = 512
TS_SEQ = 256

_LOG2E = 1.4426950408889634
_SEL_SHIFT = NSA_SEL_LEN.bit_length() - 1
_CHUNK_SHIFT = GLA_CHUNK.bit_length() - 1

_NT = (((1,), (1,)), ((), ()))
_TN = (((0,), (0,)), ((), ()))


def _params(*sem):
    return pltpu.CompilerParams(dimension_semantics=sem, vmem_limit_bytes=VMEM_LIMIT)


def _sigmoid(x):
    return 1.0 / (1.0 + jnp.exp(-x))


def _rms_bf16(x, g):
    ms = jnp.mean(x * x, axis=-1, keepdims=True)
    return (x * lax.rsqrt(ms + NORM_EPS) * g).astype(BF)


def _proj_kernel(x_ref, g_ref, w_ref, wa_ref, o_ref, oa_ref, xn_ref):
    @pl.when(pl.program_id(1) == 0)
    def _():
        xn = _rms_bf16(x_ref[...], g_ref[...])
        xn_ref[...] = xn
        oa_ref[...] = jnp.dot(xn, wa_ref[...], preferred_element_type=F32)

    o_ref[...] = jnp.dot(xn_ref[...], w_ref[...], preferred_element_type=F32).astype(o_ref.dtype)


def _proj(x, gain, w, w_aux, *, tm, tn):
    M, K = x.shape
    N = w.shape[1]
    Na = w_aux.shape[1]
    return pl.pallas_call(
        _proj_kernel,
        grid=(M // tm, N // tn),
        in_specs=[
            pl.BlockSpec((tm, K), lambda i, j: (i, 0)),
            pl.BlockSpec((1, K), lambda i, j: (0, 0)),
            pl.BlockSpec((K, tn), lambda i, j: (0, j)),
            pl.BlockSpec((K, Na), lambda i, j: (0, 0), pipeline_mode=pl.Buffered(1)),
        ],
        out_specs=[
            pl.BlockSpec((tm, tn), lambda i, j: (i, j)),
            pl.BlockSpec((tm, Na), lambda i, j: (i, 0)),
        ],
        out_shape=[jax.ShapeDtypeStruct((M, N), BF), jax.ShapeDtypeStruct((M, Na), F32)],
        scratch_shapes=[pltpu.VMEM((tm, K), BF)],
        compiler_params=_params("parallel", "arbitrary"),
        name="proj",
    )(x, gain.reshape(1, K), w, w_aux)


def _glu_kernel(x_ref, g_ref, wa_ref, wg_ref, o_ref, xn_ref):
    @pl.when(pl.program_id(1) == 0)
    def _():
        xn_ref[...] = _rms_bf16(x_ref[...], g_ref[...])

    xn = xn_ref[...]
    a = jnp.dot(xn, wa_ref[...], preferred_element_type=F32)
    g = jnp.dot(xn, wg_ref[...], preferred_element_type=F32)
    o_ref[...] = a * _sigmoid(g)


def _glu_proj(x, gain, w, *, tm, tn):
    M, K = x.shape
    C = w.shape[1] // 2
    nj = C // tn
    return pl.pallas_call(
        _glu_kernel,
        grid=(M // tm, nj),
        in_specs=[
            pl.BlockSpec((tm, K), lambda i, j: (i, 0)),
            pl.BlockSpec((1, K), lambda i, j: (0, 0)),
            pl.BlockSpec((K, tn), lambda i, j: (0, j)),
            pl.BlockSpec((K, tn), lambda i, j: (0, j + nj)),
        ],
        out_specs=pl.BlockSpec((tm, tn), lambda i, j: (i, j)),
        out_shape=jax.ShapeDtypeStruct((M, C), F32),
        scratch_shapes=[pltpu.VMEM((tm, K), BF)],
        compiler_params=_params("parallel", "arbitrary"),
        name="glu_proj",
    )(x, gain.reshape(1, K), w, w)


def _outproj_kernel(a_ref, w_ref, h_ref, o_ref):
    o_ref[...] = h_ref[...] + jnp.dot(a_ref[...], w_ref[...], preferred_element_type=F32)


def _outproj(a, w, h, *, tm):
    M, K = a.shape
    N = w.shape[1]
    return pl.pallas_call(
        _outproj_kernel,
        grid=(M // tm,),
        in_specs=[
            pl.BlockSpec((tm, K), lambda i: (i, 0)),
            pl.BlockSpec((K, N), lambda i: (0, 0)),
            pl.BlockSpec((tm, N), lambda i: (i, 0)),
        ],
        out_specs=pl.BlockSpec((tm, N), lambda i: (i, 0)),
        out_shape=jax.ShapeDtypeStruct((M, N), F32),
        compiler_params=_params("parallel"),
        name="outproj",
    )(a, w, h)


def _ffn_kernel(x_ref, g_ref, w1_ref, w2_ref, o_ref, xn_ref):
    @pl.when(pl.program_id(1) == 0)
    def _():
        x = x_ref[...]
        xn_ref[...] = _rms_bf16(x, g_ref[...])
        o_ref[...] = x

    a = jnp.dot(xn_ref[...], w1_ref[...], preferred_element_type=F32)
    a = jnp.square(jnp.maximum(a, 0.0)).astype(BF)
    o_ref[...] += jnp.dot(a, w2_ref[...], preferred_element_type=F32)


def _ffn(x, gain, w1, w2, layer, *, tm, tf):
    M, K = x.shape
    Fd = w1.shape[2]
    return pl.pallas_call(
        _ffn_kernel,
        grid=(M // tm, Fd // tf),
        in_specs=[
            pl.BlockSpec((tm, K), lambda i, j: (i, 0)),
            pl.BlockSpec((1, K), lambda i, j: (0, 0)),
            pl.BlockSpec((None, K, tf), lambda i, j: (layer, 0, j)),
            pl.BlockSpec((None, tf, K), lambda i, j: (layer, j, 0)),
        ],
        out_specs=pl.BlockSpec((tm, K), lambda i, j: (i, 0)),
        out_shape=jax.ShapeDtypeStruct((M, K), F32),
        scratch_shapes=[pltpu.VMEM((tm, K), BF)],
        compiler_params=_params("parallel", "arbitrary"),
        name="ffn",
    )(x, gain.reshape(1, K), w1, w2)


def _ple_kernel(h_ref, p_ref, g_ref, wg_ref, wp_ref, gf_ref, o_ref, *, final):
    h = h_ref[...]
    gate = _sigmoid(jnp.dot(_rms_bf16(h, g_ref[...]), wg_ref[...], preferred_element_type=F32))
    proj = jnp.dot(p_ref[...].astype(BF), wp_ref[...], preferred_element_type=F32)
    out = h + proj * gate
    if final:
        ms = jnp.mean(out * out, axis=-1, keepdims=True)
        out = out * lax.rsqrt(ms + NORM_EPS) * gf_ref[...]
    o_ref[...] = out


def _ple(h, p, gain, w_gate, w_proj, layer, gain_final, *, final, tm):
    M, K = h.shape
    Pd = p.shape[2]
    return pl.pallas_call(
        functools.partial(_ple_kernel, final=final),
        grid=(M // tm,),
        in_specs=[
            pl.BlockSpec((tm, K), lambda i: (i, 0)),
            pl.BlockSpec((None, tm, Pd), lambda i: (layer, i, 0)),
            pl.BlockSpec((1, K), lambda i: (0, 0)),
            pl.BlockSpec((None, K, K), lambda i: (layer, 0, 0), pipeline_mode=pl.Buffered(1)),
            pl.BlockSpec((None, Pd, K), lambda i: (layer, 0, 0), pipeline_mode=pl.Buffered(1)),
            pl.BlockSpec((1, K), lambda i: (0, 0)),
        ],
        out_specs=pl.BlockSpec((tm, K), lambda i: (i, 0)),
        out_shape=jax.ShapeDtypeStruct((M, K), F32),
        compiler_params=_params("parallel"),
        name="ple",
    )(h, p, gain.reshape(1, K), w_gate, w_proj, gain_final.reshape(1, K))


def _compress_kernel(x_ref, pos_ref, w1_ref, w2_ref, o_ref):
    Dh = x_ref.shape[-1]
    R = x_ref.shape[0] // NSA_CMP_STRIDE
    nl = NSA_CMP_STRIDE
    a = jnp.zeros((R, w1_ref.shape[-1]), F32)
    b = jnp.zeros((R, w1_ref.shape[-1]), F32)
    for l in range(nl):
        xl = x_ref[pl.ds(l, R, stride=nl), :]
        a = a + jnp.dot((xl + pos_ref[0, l:l + 1, :]).astype(BF),
                        w1_ref[0, l * Dh:(l + 1) * Dh, :], preferred_element_type=F32)
        b = b + jnp.dot((xl + pos_ref[0, nl + l:nl + l + 1, :]).astype(BF),
                        w1_ref[0, (nl + l) * Dh:(nl + l + 1) * Dh, :], preferred_element_type=F32)
    hid = a + pltpu.roll(b, R - 1, 0)
    hid = hid * _sigmoid(hid)
    out = jnp.dot(hid.astype(BF), w2_ref[0], preferred_element_type=F32)
    row = lax.broadcasted_iota(jnp.int32, out.shape, 0)
    o_ref[0, 0, 0] = jnp.where(row < R - 1, out, 0.0).astype(BF)


def _compress(aux, pos, w1, w2, *, B, S, col0):
    G, Dh = NSA_KV_HEADS, NSA_DH
    R = S // NSA_CMP_STRIDE
    return pl.pallas_call(
        _compress_kernel,
        grid=(B, 2, G),
        in_specs=[
            pl.BlockSpec((S, Dh), lambda b, c, g: (b, col0 + c * G + g)),
            pl.BlockSpec((1,) + pos.shape[1:], lambda b, c, g: (c, 0, 0)),
            pl.BlockSpec((1,) + w1.shape[1:], lambda b, c, g: (c, 0, 0)),
            pl.BlockSpec((1,) + w2.shape[1:], lambda b, c, g: (c, 0, 0)),
        ],
        out_specs=pl.BlockSpec((1, 1, 1, R, Dh), lambda b, c, g: (b, c, g, 0, 0)),
        out_shape=jax.ShapeDtypeStruct((B, 2, G, R, Dh), BF),
        compiler_params=_params("parallel", "parallel", "parallel"),
        name="nsa_compress",
    )(aux, pos, w1, w2)


def _nsa_kernel(q_ref, gt_ref, kc_ref, vc_ref, ks_ref, vs_ref, kw_ref, vw_ref, ovt_ref, o_ref,
                ksa_ref, vst_ref, vwt_ref, m_ref, acc_ref, sa_ref, sb_ref, ma_ref, mb_ref, ow_ref, *, tq, tk, n_top):
    Dh = NSA_DH
    i = pl.program_id(2)
    t0 = i * tq
    S = ks_ref.shape[0]
    tc = 512

    @pl.when(i == 0)
    def _():
        lane = lax.broadcasted_iota(jnp.int32, (tc, LANE), 1)
        krow = lax.broadcasted_iota(jnp.int32, (tc, LANE), 0)
        ones_rows = jnp.where(lax.broadcasted_iota(jnp.int32, (SUM_ROWS, tc), 0) == 0, 1.0, 0.0).astype(BF)
        for c in range(S // tc):
            sl = slice(c * tc, (c + 1) * tc)
            vst_ref[0:Dh, sl] = vs_ref[sl, :].astype(F32).T.astype(BF)
            vst_ref[Dh:Dh + SUM_ROWS, sl] = ones_rows
            vwt_ref[0:Dh, sl] = vw_ref[sl, :].astype(F32).T.astype(BF)
            vwt_ref[Dh:Dh + SUM_ROWS, sl] = ones_rows
            ksa_ref[sl, 0:Dh] = ks_ref[sl, :]
            ksa_ref[sl, Dh:Dh + LANE] = jnp.where(
                jnp.right_shift(c * tc + krow, _SEL_SHIFT) == lane, 1.0, 0.0).astype(BF)

    H = NSA_HPG
    tcol = t0 + lax.broadcasted_iota(jnp.int32, (1, tq), 1)
    qst = jnp.concatenate([q_ref[:, h * Dh:(h + 1) * Dh] for h in range(H)], axis=0)
    gtT = _sigmoid(gt_ref[...]).T

    def heads(x):
        return jnp.concatenate([x] * H, axis=1)

    kc = kc_ref[0, 0, 0]
    vcT = vc_ref[0, 0, 0].astype(F32).T.astype(BF)
    ncp = kc.shape[0]
    n_row = lax.broadcasted_iota(jnp.int32, (ncp, tq), 0)
    valid_c = heads((n_row * NSA_CMP_STRIDE + (NSA_CMP_LEN - 1)) <= tcol)
    s = lax.dot_general(kc, qst, _NT, preferred_element_type=F32)
    s = jnp.where(valid_c, s, NEG)
    e = jnp.exp2(s - jnp.max(s, axis=0, keepdims=True))
    den = jnp.sum(e, axis=0, keepdims=True)
    p = jnp.where(valid_c, e, 0.0) * (1.0 / den)
    o_c = jnp.dot(vcT, p.astype(BF), preferred_element_type=F32)
    psum = p[:, 0:tq]
    for h in range(1, H):
        psum = psum + p[:, h * tq:(h + 1) * tq]

    hw = (H // 2) * tq

    wlen = NSA_WINDOW + tq
    ws = pl.multiple_of(jnp.maximum(t0 - NSA_WINDOW, 0), tq)
    kwt = kw_ref[pl.ds(ws, wlen), :]
    vwT = vwt_ref[:, pl.ds(ws, wlen)]
    kpos_w = ws + lax.broadcasted_iota(jnp.int32, (wlen, tq), 0)
    ok_w = (kpos_w <= tcol) & (kpos_w > tcol - NSA_WINDOW)
    bias_w = jnp.where(ok_w, 0.0, NEG)
    bias_w2 = jnp.concatenate([bias_w] * (H // 2), axis=1)
    sw = [lax.dot_general(kwt, qst[c * hw:(c + 1) * hw], _NT, preferred_element_type=F32) + bias_w2
          for c in range(2)]
    o_w = []
    for c in range(2):
        e = jnp.exp2(sw[c] - jnp.max(sw[c], axis=0, keepdims=True)).astype(BF)
        ow = jnp.dot(vwT, e, preferred_element_type=F32)
        o_w.append(ow[0:Dh] * (1.0 / ow[Dh:Dh + 1]))
    ow_ref[...] = jnp.concatenate(o_w, axis=1)

    ovt = ovt_ref[...]
    p_hi = psum.astype(BF)
    p_lo = (psum - p_hi.astype(F32)).astype(BF)
    imp = (jnp.dot(ovt, p_hi, preferred_element_type=F32)
           + jnp.dot(ovt, p_lo, preferred_element_type=F32))
    n_sel = imp.shape[0]
    m_idx = lax.broadcasted_iota(jnp.int32, imp.shape, 0)
    m_idx_f = m_idx.astype(F32)
    diff = jnp.right_shift(tcol, _SEL_SHIFT) - m_idx
    forced = (m_idx == 0) | ((diff >= 0) & (diff < NSA_SEL_LOCAL))
    member = jnp.where(forced, 1.0, 0.0)
    work = jnp.where(forced, -3e38, jnp.where(diff >= 0, imp, -NSA_FORCE))
    for _ in range(n_top - 1 - NSA_SEL_LOCAL):
        mx = jnp.max(work, axis=0, keepdims=True)
        first = jnp.min(jnp.where(work == mx, m_idx_f, 1e6), axis=0, keepdims=True)
        pick = m_idx_f == first
        member = jnp.where(pick, 1.0, member)
        work = jnp.where(pick, -3e38, work)
    sel_bias = jnp.where((member > 0.5) & (diff >= 0), 0.0, NEG)
    sel_bias = jnp.concatenate([sel_bias, jnp.zeros((LANE - n_sel, tq), F32)], axis=0)
    sb = sel_bias.T.astype(BF)
    q_aug = jnp.concatenate([qst, jnp.concatenate([sb] * H, axis=0)], axis=1)

    m_ref[...] = jnp.full(m_ref.shape, NEG, F32)
    acc_ref[...] = jnp.zeros(acc_ref.shape, F32)
    key_row = lax.broadcasted_iota(jnp.int32, (tk, tq), 0)

    def produce(j, s_ref, mt_ref, diagonal):
        k0 = pl.multiple_of(j * tk, tk)
        s = lax.dot_general(ksa_ref[pl.ds(k0, tk), :], q_aug, _NT, preferred_element_type=F32)
        if diagonal:
            s = jnp.where(heads((k0 + key_row) <= tcol), s, NEG)
        s_ref[...] = s
        mt_ref[...] = jnp.max(s, axis=0, keepdims=True)

    def consume(j, s_ref, mt_ref):
        k0 = pl.multiple_of(j * tk, tk)
        va = vst_ref[:, pl.ds(k0, tk)]
        for c in range(2):
            sl = slice(c * hw, (c + 1) * hw)
            m_prev = m_ref[:, sl]
            m_new = jnp.maximum(m_prev, mt_ref[:, sl])
            alpha = jnp.exp2(m_prev - m_new)
            e = jnp.exp2(s_ref[:, sl] - m_new).astype(BF)
            acc_ref[:, sl] = alpha * acc_ref[:, sl] + jnp.dot(va, e, preferred_element_type=F32)
            m_ref[:, sl] = m_new

    n_full = (t0 + tq + tk - 1) // tk - 1
    start = n_full % 2

    @pl.when(n_full == 0)
    def _():
        produce(0, sa_ref, ma_ref, True)

    @pl.when((n_full > 0) & (start == 0))
    def _():
        produce(0, sa_ref, ma_ref, False)

    @pl.when(n_full == 1)
    def _():
        produce(0, sb_ref, mb_ref, False)
        produce(1, sa_ref, ma_ref, True)
        consume(0, sb_ref, mb_ref)

    @pl.when((n_full > 1) & (start == 1))
    def _():
        produce(0, sb_ref, mb_ref, False)
        produce(1, sa_ref, ma_ref, False)
        consume(0, sb_ref, mb_ref)

    def sel_pair(jj, carry):
        ta = start + 2 * jj
        produce(ta + 1, sb_ref, mb_ref, False)
        consume(ta, sa_ref, ma_ref)
        produce(ta + 2, sa_ref, ma_ref, False)
        consume(ta + 1, sb_ref, mb_ref)
        return carry

    n_pairs = n_full // 2
    lax.fori_loop(0, n_pairs - 1, sel_pair, 0)

    @pl.when(n_pairs > 0)
    def _():
        ta = n_full - 2
        produce(ta + 1, sb_ref, mb_ref, False)
        consume(ta, sa_ref, ma_ref)
        produce(ta + 2, sa_ref, ma_ref, True)
        consume(ta + 1, sb_ref, mb_ref)

    consume(n_full, sa_ref, ma_ref)

    o_w = ow_ref[...]
    o_s = acc_ref[0:Dh, :] * (1.0 / acc_ref[Dh:Dh + 1, :])

    for h in range(H):
        sl = slice(h * tq, (h + 1) * tq)
        out = (gtT[3 * h:3 * h + 1, :] * o_c[:, sl] + gtT[3 * h + 1:3 * h + 2, :] * o_s[:, sl]
               + gtT[3 * h + 2:3 * h + 3, :] * o_w[:, sl])
        o_ref[:, h * Dh:(h + 1) * Dh] = out.T.astype(BF)


def _nsa_core(P, gates, cmp_kv, ovt, *, B, S, tq, tk):
    G, HPG, Dh = NSA_KV_HEADS, NSA_HPG, NSA_DH
    nq = S // tq
    ncp = cmp_kv.shape[3]
    n_sel = S // NSA_SEL_LEN
    n_top = min(NSA_SEL_TOPK, n_sel)
    assert S >= NSA_WINDOW + tq and NSA_WINDOW % tq == 0 and S % tk == 0 and tk % NSA_SEL_LEN == 0
    assert S % 512 == 0
    qcols = NSA_HEADS * Dh // LANE
    kvcols = G

    def kv_spec(slab):
        off = qcols + slab * kvcols
        return pl.BlockSpec((S, Dh), lambda b, g, i: (b, off + g))

    return pl.pallas_call(
        functools.partial(_nsa_kernel, tq=tq, tk=tk, n_top=n_top),
        grid=(B, G, nq),
        in_specs=[
            pl.BlockSpec((tq, HPG * Dh), lambda b, g, i: (b * nq + i, g)),
            pl.BlockSpec((tq, LANE), lambda b, g, i: (b * nq + i, g)),
            pl.BlockSpec((1, 1, 1, ncp, Dh), lambda b, g, i: (b, 0, g, 0, 0)),
            pl.BlockSpec((1, 1, 1, ncp, Dh), lambda b, g, i: (b, 1, g, 0, 0)),
            kv_spec(0), kv_spec(1), kv_spec(2), kv_spec(3),
            pl.BlockSpec(ovt.shape, lambda b, g, i: (0, 0)),
        ],
        out_specs=pl.BlockSpec((tq, HPG * Dh), lambda b, g, i: (b * nq + i, g)),
        out_shape=jax.ShapeDtypeStruct((B * S, NSA_HEADS * Dh), BF),
        scratch_shapes=[
            pltpu.VMEM((S, Dh + LANE), BF),
            pltpu.VMEM((Dh + SUM_ROWS, S), BF),
            pltpu.VMEM((Dh + SUM_ROWS, S), BF),
            pltpu.VMEM((1, HPG * tq), F32),
            pltpu.VMEM((Dh + SUM_ROWS, HPG * tq), F32),
            pltpu.VMEM((tk, HPG * tq), F32),
            pltpu.VMEM((tk, HPG * tq), F32),
            pltpu.VMEM((1, HPG * tq), F32),
            pltpu.VMEM((1, HPG * tq), F32),
            pltpu.VMEM((Dh, HPG * tq), F32),
        ],
        compiler_params=_params("parallel", "parallel", "arbitrary"),
        name="nsa_core",
    )(P, gates, cmp_kv, cmp_kv, P, P, P, P, ovt)


def _overlap_matrix_t(S, ncp):
    n_cmp = (S - NSA_CMP_LEN) // NSA_CMP_STRIDE + 1
    n_sel = S // NSA_SEL_LEN
    cs = np.arange(n_cmp) * NSA_CMP_STRIDE
    ss = np.arange(n_sel) * NSA_SEL_LEN
    ovl = ((cs[:, None] <= ss[None, :] + NSA_SEL_LEN - 1)
           & (cs[:, None] + NSA_CMP_LEN - 1 >= ss[None, :])).astype(np.float32)
    out = np.zeros((n_sel, ncp), np.float32)
    out[:, :n_cmp] = ovl.T
    return jnp.asarray(out, BF)


def _nsa_layer(h, gain, w_in, w_out, pos_k, w1_k, w2_k, pos_v, w1_v, w2_v, *, B, S):
    G, HPG, Dh = NSA_KV_HEADS, NSA_HPG, NSA_DH
    D = h.shape[1]
    qw, kvw = NSA_HEADS * Dh, G * Dh
    main = qw + 6 * kvw
    w_main = jnp.concatenate([w_in[:, :qw] * (Dh ** -0.5 * _LOG2E), w_in[:, qw + 2 * kvw:main]],
                             axis=1).astype(BF)
    wg = w_in[:, main:].reshape(D, G, 3 * HPG)
    wg = jnp.pad(wg, ((0, 0), (0, 0), (0, LANE - 3 * HPG))).reshape(D, G * LANE)
    w_aux = jnp.concatenate([wg, w_in[:, qw:qw + 2 * kvw]], axis=1).astype(BF)
    P, aux = _proj(h, gain, w_main, w_aux, tm=TM, tn=TN_PROJ)

    pos = jnp.stack([pos_k, pos_v])
    w1 = jnp.stack([w1_k, w1_v]).astype(BF)
    w2 = jnp.stack([w2_k, w2_v]).astype(BF)
    cmp_kv = _compress(aux, pos, w1, w2, B=B, S=S, col0=G)

    ovt = _overlap_matrix_t(S, S // NSA_CMP_STRIDE)
    o = _nsa_core(P, aux, cmp_kv, ovt, B=B, S=S, tq=TQ_NSA, tk=TK_NSA)
    return _outproj(o, w_out.astype(BF), h, tm=TM)


def _conv_kernel(u_ref, halo_ref, dw_ref, db_ref, lg_ref, lb_ref, w_ref, h_ref, o_ref,
                 buf_ref, sh_ref, y_ref, *, ts):
    C = u_ref.shape[-1]
    halo = halo_ref[...]
    buf_ref[0:CONV_HALO, :] = jnp.where(pl.program_id(1) > 0, halo, 0.0)
    buf_ref[CONV_HALO:CONV_HALO + ts, :] = u_ref[...]
    off = CONV_HALO - (CONV_WIDTH - 1)
    rs = 128
    sub = 8
    span = ts + CONV_HALO - sub

    def chan_body(c, carry):
        c0 = pl.multiple_of(c * LANE, LANE)
        for r in range(1, sub):
            sh_ref[r, 0:span, :] = buf_ref[pl.ds(r, span), pl.ds(c0, LANE)]
        for rr in range(ts // rs):
            acc = jnp.zeros((rs, LANE), F32) + db_ref[:, pl.ds(c0, LANE)]
            for k in range(CONV_WIDTH):
                r = (off + k) % sub
                base = rr * rs + off + k - r
                if r == 0:
                    win = buf_ref[pl.ds(base, rs), pl.ds(c0, LANE)]
                else:
                    win = sh_ref[r, pl.ds(base, rs), :]
                acc = acc + win * dw_ref[k:k + 1, pl.ds(c0, LANE)]
            y_ref[pl.ds(rr * rs, rs), pl.ds(c0, LANE)] = acc
        return carry

    lax.fori_loop(0, C // LANE, chan_body, 0)

    y = y_ref[...]
    mu = jnp.mean(y, axis=-1, keepdims=True)
    yc = y - mu
    var = jnp.mean(yc * yc, axis=-1, keepdims=True)
    z = yc * lax.rsqrt(var + LN_EPS) * lg_ref[...] + lb_ref[...]
    z = (z * _sigmoid(z)).astype(BF)
    o_ref[...] = h_ref[...] + jnp.dot(z, w_ref[...], preferred_element_type=F32)


def _conv_tail(u, dw, db, ln_g, ln_b, w_out, h, *, B, S, ts):
    M, C = u.shape
    D = w_out.shape[1]
    ns = S // ts
    hb = ts // CONV_HALO
    dwp = jnp.pad(dw, ((0, CONV_HALO - CONV_WIDTH), (0, 0)))
    return pl.pallas_call(
        functools.partial(_conv_kernel, ts=ts),
        grid=(B, ns),
        in_specs=[
            pl.BlockSpec((ts, C), lambda b, i: (b * ns + i, 0)),
            pl.BlockSpec((CONV_HALO, C), lambda b, i: (jnp.maximum((b * ns + i) * hb - 1, 0), 0)),
            pl.BlockSpec((CONV_HALO, C), lambda b, i: (0, 0)),
            pl.BlockSpec((1, C), lambda b, i: (0, 0)),
            pl.BlockSpec((1, C), lambda b, i: (0, 0)),
            pl.BlockSpec((1, C), lambda b, i: (0, 0)),
            pl.BlockSpec((C, D), lambda b, i: (0, 0)),
            pl.BlockSpec((ts, D), lambda b, i: (b * ns + i, 0)),
        ],
        out_specs=pl.BlockSpec((ts, D), lambda b, i: (b * ns + i, 0)),
        out_shape=jax.ShapeDtypeStruct((M, D), F32),
        scratch_shapes=[pltpu.VMEM((CONV_HALO + ts, C), F32), pltpu.VMEM((8, CONV_HALO + ts, LANE), F32),
                        pltpu.VMEM((ts, C), F32)],
        compiler_params=_params("parallel", "parallel"),
        name="conv_tail",
    )(u, u, dwp, db.reshape(1, C), ln_g.reshape(1, C), ln_b.reshape(1, C), w_out, h)


def _conv_layer(h, gain, w_in, dw, db, ln_g, ln_b, w_out, *, B, S):
    u = _glu_proj(h, gain, w_in.astype(BF), tm=TM, tn=TN_GLU)
    return _conv_tail(u, dw, db, ln_g, ln_b, w_out.astype(BF), h, B=B, S=S, ts=TS_SEQ)


def _gla_kernel(q_ref, k_ref, v_ref, r_ref, gz_ref, wgu_ref, bg_ref, ng_ref, o_ref, st_ref, *, ts, nh):
    C = GLA_CHUNK
    dk, dv = GLA_DK, GLA_DV

    @pl.when(pl.program_id(2) == 0)
    def _():
        st_ref[...] = jnp.zeros(st_ref.shape, F32)

    z = jnp.dot(gz_ref[...].astype(BF), wgu_ref[...], preferred_element_type=F32) + bg_ref[...]
    glog = (jnp.minimum(z, 0.0) - jnp.log(1.0 + jnp.exp(-jnp.abs(z)))) / GLA_GATE_NORM

    g1 = glog.astype(BF)
    r1 = glog - g1.astype(F32)
    g2 = r1.astype(BF)
    g3 = (r1 - g2.astype(F32)).astype(BF)
    row = lax.broadcasted_iota(jnp.int32, (ts, ts), 0)
    col = lax.broadcasted_iota(jnp.int32, (ts, ts), 1)
    same = jnp.right_shift(row, _CHUNK_SHIFT) == jnp.right_shift(col, _CHUNK_SHIFT)
    tri = jnp.where(same & (col <= row), 1.0, 0.0).astype(BF)
    full = jnp.where(same, 1.0, 0.0).astype(BF)

    def msum(mat):
        return (jnp.dot(mat, g1, preferred_element_type=F32)
                + jnp.dot(mat, g2, preferred_element_type=F32)
                + jnp.dot(mat, g3, preferred_element_type=F32))

    bcum = msum(tri)
    blast = msum(full)

    q = q_ref[...].astype(F32) * (dk ** -0.5)
    k = k_ref[...].astype(F32)
    q_dec = (q * jnp.exp(bcum)).astype(BF)
    k_intra = (k * jnp.exp(-bcum)).astype(BF)
    k_state = (k * jnp.exp(blast - bcum)).astype(BF)
    decay = jnp.exp(blast)
    causal = (lax.broadcasted_iota(jnp.int32, (C, C), 1) <= lax.broadcasted_iota(jnp.int32, (C, C), 0))

    outs = [[] for _ in range(nh)]
    for c in range(ts // C):
        sl = slice(c * C, (c + 1) * C)
        for hh in range(nh):
            ka = slice(hh * dk, (hh + 1) * dk)
            vc = v_ref[sl, hh * dv:(hh + 1) * dv]
            a = lax.dot_general(q_dec[sl, ka], k_intra[sl, ka], _NT, preferred_element_type=F32)
            a = jnp.where(causal, a, 0.0).astype(BF)
            st = st_ref[hh]
            o = (jnp.dot(a, vc, preferred_element_type=F32)
                 + lax.dot_general(q_dec[sl, ka], st.astype(BF), _NT, preferred_element_type=F32))
            outs[hh].append(o)
            st_ref[hh] = st * decay[c * C:c * C + 1, ka] + lax.dot_general(
                vc, k_state[sl, ka], _TN, preferred_element_type=F32)

    for hh in range(nh):
        o = jnp.concatenate(outs[hh], axis=0)
        o = o * lax.rsqrt(jnp.mean(o * o, axis=-1, keepdims=True) + NORM_EPS) * ng_ref[...]
        r = r_ref[:, hh * dv:(hh + 1) * dv].astype(F32)
        o_ref[:, hh * dv:(hh + 1) * dv] = (o * (r * _sigmoid(r))).astype(BF)


def _gla_core(P, gz, wgu, bg, ng, *, B, S, ts, nh):
    H, dk, dv = GLA_HEADS, GLA_DK, GLA_DV
    ns = S // ts
    hg = H // nh
    kq = hg
    vv = 2 * H * dk // (nh * dv)
    rr = vv + hg
    return pl.pallas_call(
        functools.partial(_gla_kernel, ts=ts, nh=nh),
        grid=(B, hg, ns),
        in_specs=[
            pl.BlockSpec((ts, nh * dk), lambda b, hh, i: (b * ns + i, hh)),
            pl.BlockSpec((ts, nh * dk), lambda b, hh, i: (b * ns + i, kq + hh)),
            pl.BlockSpec((ts, nh * dv), lambda b, hh, i: (b * ns + i, vv + hh)),
            pl.BlockSpec((ts, nh * dv), lambda b, hh, i: (b * ns + i, rr + hh)),
            pl.BlockSpec((ts, LANE), lambda b, hh, i: (b * ns + i, 0)),
            pl.BlockSpec((LANE, nh * dk), lambda b, hh, i: (0, hh)),
            pl.BlockSpec((1, nh * dk), lambda b, hh, i: (0, hh)),
            pl.BlockSpec((1, dv), lambda b, hh, i: (0, 0)),
        ],
        out_specs=pl.BlockSpec((ts, nh * dv), lambda b, hh, i: (b * ns + i, hh)),
        out_shape=jax.ShapeDtypeStruct((B * S, H * dv), BF),
        scratch_shapes=[pltpu.VMEM((nh, dv, dk), F32)],
        compiler_params=_params("parallel", "parallel", "arbitrary"),
        name="gla_core",
    )(P, P, P, P, gz, wgu, bg.reshape(1, H * dk), ng.reshape(1, dv))


def _gla_layer(h, gain, w_in, w_gate_up, b_gate, norm_g, w_out, *, B, S):
    H, dk, dv = GLA_HEADS, GLA_DK, GLA_DV
    main = 2 * H * dk + 2 * H * dv
    w_main = w_in[:, :main].astype(BF)
    w_gz = jnp.pad(w_in[:, main:], ((0, 0), (0, LANE - GLA_GATE_RANK))).astype(BF)
    P, gz = _proj(h, gain, w_main, w_gz, tm=TM, tn=TN_PROJ)
    wgu = jnp.pad(w_gate_up, ((0, LANE - GLA_GATE_RANK), (0, 0))).astype(BF)
    o = _gla_core(P, gz, wgu, b_gate, norm_g, B=B, S=S, ts=TS_SEQ, nh=GLA_HEADS)
    return _outproj(o, w_out.astype(BF), h, tm=TM)


def kernel(x, p, norm_mix, norm_ffn, norm_ple, norm_final, ffn_w1, ffn_w2, ple_w_proj, ple_w_gate,
           nsa_w_in, nsa_w_out, nsa_cmp_pos_k, nsa_cmp_w1_k, nsa_cmp_w2_k, nsa_cmp_pos_v,
           nsa_cmp_w1_v, nsa_cmp_w2_v, conv_w_in, conv_dw, conv_db, conv_ln_g, conv_ln_b,
           conv_w_out, gla_w_in, gla_w_gate_up, gla_b_gate, gla_norm_g, gla_w_out):
    B, S, D = x.shape
    depth = p.shape[0]
    M = B * S
    h = x.reshape(M, D)
    p2 = p.reshape(depth, M, p.shape[-1])
    ffn_w1, ffn_w2 = ffn_w1.astype(BF), ffn_w2.astype(BF)
    ple_w_gate, ple_w_proj = ple_w_gate.astype(BF), ple_w_proj.astype(BF)
    for i in range(depth):
        m, j = i % 3, i // 3
        if m == 0:
            h = _nsa_layer(h, norm_mix[i], nsa_w_in[j], nsa_w_out[j], nsa_cmp_pos_k[j],
                           nsa_cmp_w1_k[j], nsa_cmp_w2_k[j], nsa_cmp_pos_v[j], nsa_cmp_w1_v[j],
                           nsa_cmp_w2_v[j], B=B, S=S)
        elif m == 1:
            h = _conv_layer(h, norm_mix[i], conv_w_in[j], conv_dw[j], conv_db[j], conv_ln_g[j],
                            conv_ln_b[j], conv_w_out[j], B=B, S=S)
        else:
            h = _gla_layer(h, norm_mix[i], gla_w_in[j], gla_w_gate_up[j], gla_b_gate[j],
                           gla_norm_g[j], gla_w_out[j], B=B, S=S)
        h = _ffn(h, norm_ffn[i], ffn_w1, ffn_w2, i, tm=TM_FFN, tf=TF_FFN)
        h = _ple(h, p2, norm_ple[i], ple_w_gate, ple_w_proj, i, norm_final,
                 final=(i == depth - 1), tm=TM_PLE)
    return h.reshape(B, S, D)
```
